```python
import math
import jax, jax.numpy as jnp
from jax import lax
import numpy as np

D_MODEL = 2048
BATCH = 4
SEQ = 2048
DEPTH = 2
DEC_BATCH = 128
DEC_SEQ = 1
PAST_LEN = 16384
PAGE_SIZE = 128

F32 = jnp.float32
CONV_WIDTH = 4
CHUNK = 64
DN_HEAD_K = 128
DN_HEAD_V = 128
DN_K_HEADS = D_MODEL // 128
DN_V_HEADS = 2 * DN_K_HEADS
DN_QK_WIDTH = DN_K_HEADS * DN_HEAD_K
DN_V_WIDTH = DN_V_HEADS * DN_HEAD_V
DN_CONV_DIM = 2 * DN_QK_WIDTH + DN_V_WIDTH
SSM_INNER = 2 * D_MODEL
SSM_HEAD_DIM = 64
SSM_HEADS = SSM_INNER // SSM_HEAD_DIM
SSM_GROUPS = 8
SSM_D_STATE = 128
SSM_CONV_DIM = SSM_INNER + 2 * SSM_GROUPS * SSM_D_STATE
N_MEM = 256
MEM_HEADS = 4
MEM_HEAD_DIM = D_MODEL // MEM_HEADS
N_EXPERTS = 64
N_EXPERT_GROUPS = 8
TOPK_GROUPS = 4
TOP_K = 8
EXPERT_FF = D_MODEL // 4
SHARED_FF = D_MODEL // 4
ROUTED_SCALE = 2.5
DEEPNORM_ALPHA = (2 * DEPTH) ** 0.25
DEEPNORM_BETA = (8 * DEPTH) ** -0.25
LN_EPS = 1e-5
RMS_EPS = 1e-6
IN_SPLIT_SIZES = (DN_CONV_DIM, DN_V_WIDTH, DN_V_HEADS, DN_V_HEADS,
                  SSM_CONV_DIM, SSM_INNER, SSM_HEADS, 2 * D_MODEL)
IN_COLS = sum(IN_SPLIT_SIZES)

kernel_name = 'hybrid_deltanet_ssd_moe_decoder_step'


def layer_norm(x, g, b):
    x = x.astype(F32)
    mu = jnp.mean(x, -1, keepdims=True)
    var = jnp.mean(jnp.square(x - mu), -1, keepdims=True)
    return (x - mu) * lax.rsqrt(var + LN_EPS) * g + b


def rms(x):
    x = x.astype(F32)
    return x * lax.rsqrt(jnp.mean(x * x, -1, keepdims=True) + RMS_EPS)


def l2norm(x):
    x = x.astype(F32)
    return x * lax.rsqrt(jnp.sum(x * x, -1, keepdims=True) + 1e-6)


def causal_conv(x, buf, w, bias=None):
    l = x.shape[1]
    xp = jnp.concatenate([buf.astype(x.dtype), x], axis=1)
    y = sum(xp[:, j:j + l] * w[j] for j in range(CONV_WIDTH))
    if bias is not None:
        y = y + bias
    return jax.nn.silu(y), xp[:, l:]


def gated_delta_rule(q, k, v, g, beta, s0):
    b, l, h, dk = q.shape
    dv = v.shape[-1]
    c = min(CHUNK, l)
    pad = (-l) % c
    n = (l + pad) // c

    def blocks(t):
        t = jnp.pad(t.astype(F32), [(0, 0), (0, pad)] + [(0, 0)] * (t.ndim - 2))
        return jnp.swapaxes(jnp.moveaxis(t.reshape((b, n, c) + t.shape[2:]), 1, 0), 2, 3)

    xs = (blocks(q * dk ** -0.5), blocks(k), blocks(v), blocks(g), blocks(beta))
    tril = jnp.tril(jnp.ones((c, c), bool))
    strict = jnp.tril(jnp.ones((c, c), bool), -1)
    eye = jnp.eye(c, dtype=F32)

    def step(s, inp):
        qc, kc, vc, gc, bc = inp
        gcum = jnp.cumsum(gc, -1)
        decay = jnp.exp(jnp.where(tril, gcum[..., :, None] - gcum[..., None, :], -jnp.inf))
        kbeta = kc * bc[..., None]
        a_mat = eye + jnp.where(strict, jnp.einsum('bhik,bhjk->bhij', kbeta, kc) * decay, 0.0)
        rhs = jnp.concatenate([vc * bc[..., None], kbeta * jnp.exp(gcum)[..., None]], -1)
        sol = lax.linalg.triangular_solve(a_mat, rhs, left_side=True, lower=True, unit_diagonal=True)
        u, w = sol[..., :dv], sol[..., dv:]
        v_new = u - jnp.einsum('bhck,bhkv->bhcv', w, s)
        attn = jnp.einsum('bhik,bhjk->bhij', qc, kc) * decay
        o = (jnp.einsum('bhik,bhkv->bhiv', qc * jnp.exp(gcum)[..., None], s)
             + jnp.einsum('bhij,bhjv->bhiv', attn, v_new))
        g_last = gcum[..., -1:]
        s = (s * jnp.exp(g_last)[..., None]
             + jnp.einsum('bhck,bhcv->bhkv', kc * jnp.exp(g_last - gcum)[..., None], v_new))
        return s, o

    s_fin, os_ = lax.scan(step, s0.astype(F32), xs)
    o = jnp.transpose(os_, (1, 0, 3, 2, 4)).reshape(b, n * c, h, dv)[:, :l]
    return o, s_fin


def ssd_scan(x, dt, a, bm, cm, h0):
    b, l, nh, p = x.shape
    g, ns = bm.shape[2], bm.shape[3]
    r = nh // g
    c = min(CHUNK, l)
    pad = (-l) % c
    n = (l + pad) // c

    def blocks(t):
        t = jnp.pad(t.astype(F32), [(0, 0), (0, pad)] + [(0, 0)] * (t.ndim - 2))
        return jnp.moveaxis(t.reshape((b, n, c) + t.shape[2:]), 1, 0)

    xdt = blocks((x * dt[..., None]).reshape(b, l, g, r, p))
    da = blocks((dt * a).reshape(b, l, g, r))
    tril = jnp.tril(jnp.ones((c, c), bool))

    def step(s, inp):
        xc, ac, bc, cc = inp
        cum = jnp.moveaxis(jnp.cumsum(ac, axis=1), 1, -1)
        seg = jnp.exp(jnp.where(tril, cum[..., :, None] - cum[..., None, :], -jnp.inf))
        cb = jnp.einsum('bign,bjgn->bgij', cc, bc)
        y = (jnp.einsum('bgij,bgrij,bjgrp->bigrp', cb, seg, xc)
             + jnp.einsum('bign,bgrpn,bgri->bigrp', cc, s, jnp.exp(cum)))
        last = cum[..., -1:]
        s = (s * jnp.exp(last)[..., None]
             + jnp.einsum('bjgn,bgrj,bjgrp->bgrpn', bc, jnp.exp(last - cum), xc))
        return s, y

    s_fin, ys = lax.scan(step, h0.astype(F32).reshape(b, g, r, p, ns), (xdt, da, blocks(bm), blocks(cm)))
    y = jnp.moveaxis(ys, 0, 1).reshape(b, n * c, nh, p)[:, :l]
    return y, s_fin.reshape(b, nh, p, ns)


def token_mixer(x, dn_s, dn_buf, ssm_s, ssm_buf, w_in, dn_conv_w, dn_a_log, dn_dt_bias, dn_norm_w,
                dn_w_out, ssm_conv_w, ssm_conv_b, ssm_a_log, ssm_dt_bias, ssm_d, ssm_norm_w,
                ssm_w_out, w_o):
    b, l, _ = x.shape
    splits = [int(s) for s in np.cumsum(IN_SPLIT_SIZES)[:-1]]
    dn_qkv, dn_z, dn_b, dn_a, ssm_xbc, ssm_z, ssm_dt, gate_logits = jnp.split(x @ w_in, splits, axis=-1)

    qkv, dn_buf_new = causal_conv(dn_qkv, dn_buf, dn_conv_w)
    q, k, v = jnp.split(qkv, [DN_QK_WIDTH, 2 * DN_QK_WIDTH], axis=-1)
    rep = DN_V_HEADS // DN_K_HEADS
    q = jnp.repeat(l2norm(q.reshape(b, l, DN_K_HEADS, DN_HEAD_K)), rep, axis=2)
    k = jnp.repeat(l2norm(k.reshape(b, l, DN_K_HEADS, DN_HEAD_K)), rep, axis=2)
    v = v.reshape(b, l, DN_V_HEADS, DN_HEAD_V).astype(F32)
    beta = jax.nn.sigmoid(dn_b.astype(F32))
    g = -jnp.exp(dn_a_log.astype(F32)) * jax.nn.softplus(dn_a.astype(F32) + dn_dt_bias)
    o, dn_s_new = gated_delta_rule(q, k, v, g, beta, dn_s)
    o = rms(o) * dn_norm_w * jax.nn.silu(dn_z.reshape(b, l, DN_V_HEADS, DN_HEAD_V).astype(F32))
    y_a = o.reshape(b, l, DN_V_WIDTH) @ dn_w_out

    xbc, ssm_buf_new = causal_conv(ssm_xbc, ssm_buf, ssm_conv_w, ssm_conv_b)
    xs, bm, cm = jnp.split(xbc, [SSM_INNER, SSM_INNER + SSM_GROUPS * SSM_D_STATE], axis=-1)
    xs = xs.reshape(b, l, SSM_HEADS, SSM_HEAD_DIM).astype(F32)
    dt = jax.nn.softplus(ssm_dt.astype(F32) + ssm_dt_bias)
    y, ssm_s_new = ssd_scan(xs, dt, -jnp.exp(ssm_a_log.astype(F32)),
                            bm.reshape(b, l, SSM_GROUPS, SSM_D_STATE),
                            cm.reshape(b, l, SSM_GROUPS, SSM_D_STATE), ssm_s)
    y = (y + ssm_d[:, None] * xs).reshape(b, l, SSM_INNER) * jax.nn.silu(ssm_z.astype(F32))
    y = rms(y.reshape(b, l, SSM_GROUPS, SSM_INNER // SSM_GROUPS)).reshape(b, l, SSM_INNER) * ssm_norm_w
    y_b = y @ ssm_w_out

    g_a, g_b = jnp.split(jax.nn.sigmoid(gate_logits.astype(F32)), 2, axis=-1)
    out = (g_a * y_a + g_b * y_b) @ w_o
    return out, dn_s_new, dn_buf_new, ssm_s_new, ssm_buf_new


def mem_attend(x, mem_k, mem_v, wq, wo):
    b, l, _ = x.shape
    q = (x @ wq).reshape(b, l, MEM_HEADS, MEM_HEAD_DIM).astype(F32)
    s = jnp.einsum('blhd,bmhd->bhlm', q, mem_k.astype(F32)) * MEM_HEAD_DIM ** -0.5
    p = jax.nn.softmax(s, axis=-1)
    o = jnp.einsum('bhlm,bmhd->blhd', p, mem_v.astype(F32)).reshape(b, l, D_MODEL)
    return o @ wo


def moe(x, router_w, router_bias, w_gate, w_up, w_down, sh_gate, sh_up, sh_down):
    b, l, d = x.shape
    t = x.reshape(b * l, d)
    epg = N_EXPERTS // N_EXPERT_GROUPS
    scores = jax.nn.sigmoid((t @ router_w).astype(F32))
    sel = scores + router_bias.astype(F32)
    grp = lax.top_k(sel.reshape(-1, N_EXPERT_GROUPS, epg), 2)[0].sum(-1)
    _, gidx = lax.top_k(grp, TOPK_GROUPS)
    gmask = jax.nn.one_hot(gidx, N_EXPERT_GROUPS, dtype=F32).sum(1) > 0
    sel = jnp.where(jnp.repeat(gmask, epg, axis=-1), sel, -jnp.inf)
    _, eidx = lax.top_k(sel, TOP_K)
    wts = jnp.take_along_axis(scores, eidx, axis=-1)
    wts = wts / jnp.sum(wts, -1, keepdims=True) * ROUTED_SCALE
    gate = jnp.einsum('tk,tke->te', wts, jax.nn.one_hot(eidx, N_EXPERTS, dtype=F32))
    out = (jax.nn.silu(t @ sh_gate) * (t @ sh_up)) @ sh_down
    for gi in range(N_EXPERT_GROUPS):
        e = slice(gi * epg, (gi + 1) * epg)
        hid = jax.nn.silu(jnp.einsum('td,edf->tef', t, w_gate[e])) * jnp.einsum('td,edf->tef', t, w_up[e])
        out = out + jnp.einsum('tef,efd->td', hid * gate[:, e, None], w_down[e])
    return out.reshape(b, l, d)


def decoder_layer(x, dn_s, dn_buf, ssm_s, ssm_buf, mem_k, mem_v, mixer_w, xa_wq, xa_wo, moe_w, ln_w):
    mix, dn_s, dn_buf, ssm_s, ssm_buf = token_mixer(x, dn_s, dn_buf, ssm_s, ssm_buf, *mixer_w)
    h = layer_norm(DEEPNORM_ALPHA * x + mix, ln_w[0], ln_w[1])
    h = layer_norm(DEEPNORM_ALPHA * h + mem_attend(h, mem_k, mem_v, xa_wq, xa_wo), ln_w[2], ln_w[3])
    h = layer_norm(DEEPNORM_ALPHA * h + moe(h, *moe_w), ln_w[4], ln_w[5])
    return h, dn_s, dn_buf, ssm_s, ssm_buf


def setup_inputs(seed: int = 0) -> dict:
    key = jax.random.key(seed)
    keys = iter(jax.random.split(key, 64))

    def nrm(shape, scale=1.0):
        return jax.random.normal(next(keys), shape, F32) * scale

    def unif(shape, lo, hi):
        return jax.random.uniform(next(keys), shape, F32, lo, hi)

    def dt_bias(shape):
        dt = jnp.exp(unif(shape, math.log(1e-3), math.log(1e-1)))
        return dt + jnp.log(-jnp.expm1(-dt))

    L, d = DEPTH, D_MODEL
    return {
        'x_prompt': nrm((BATCH, SEQ, d)),
        'x_sample': nrm((DEC_BATCH, DEC_SEQ, d)),
        'mem_prompt': nrm((BATCH, N_MEM, d)),
        'state_dn': nrm((L, DEC_BATCH, DN_V_HEADS, DN_HEAD_K, DN_HEAD_V), 0.5),
        'state_dn_conv': nrm((L, DEC_BATCH, CONV_WIDTH - 1, DN_CONV_DIM)),
        'state_ssm': nrm((L, DEC_BATCH, SSM_HEADS, SSM_HEAD_DIM, SSM_D_STATE), 0.1),
        'state_ssm_conv': nrm((L, DEC_BATCH, CONV_WIDTH - 1, SSM_CONV_DIM)),
        'cache_mem_k': nrm((L, DEC_BATCH, N_MEM, MEM_HEADS, MEM_HEAD_DIM)),
        'cache_mem_v': nrm((L, DEC_BATCH, N_MEM, MEM_HEADS, MEM_HEAD_DIM), DEEPNORM_BETA),
        'w_in': nrm((L, d, IN_COLS), d ** -0.5),
        'dn_conv_w': nrm((L, CONV_WIDTH, DN_CONV_DIM), CONV_WIDTH ** -0.5),
        'dn_a_log': jnp.log(unif((L, DN_V_HEADS), 1.0, 16.0)),
        'dn_dt_bias': dt_bias((L, DN_V_HEADS)),
        'dn_norm_w': 1.0 + nrm((L, DN_HEAD_V), 0.02),
        'dn_w_out': nrm((L, DN_V_WIDTH, d), DN_V_WIDTH ** -0.5),
        'ssm_conv_w': nrm((L, CONV_WIDTH, SSM_CONV_DIM), CONV_WIDTH ** -0.5),
        'ssm_conv_b': nrm((L, SSM_CONV_DIM), 0.01),
        'ssm_a_log': jnp.log(unif((L, SSM_HEADS), 1.0, 16.0)),
        'ssm_dt_bias': dt_bias((L, SSM_HEADS)),
        'ssm_d': 1.0 + nrm((L, SSM_HEADS), 0.02),
        'ssm_norm_w': 1.0 + nrm((L, SSM_INNER), 0.02),
        'ssm_w_out': nrm((L, SSM_INNER, d), SSM_INNER ** -0.5),
        'w_o': nrm((L, d, d), d ** -0.5 * DEEPNORM_BETA),
        'ln1_g': 1.0 + nrm((L, d), 0.02),
        'ln1_b': nrm((L, d), 0.01),
        'xa_wq': nrm((L, d, d), d ** -0.5),
        'xa_wk': nrm((L, d, d), d ** -0.5),
        'xa_wv': nrm((L, d, d), d ** -0.5 * DEEPNORM_BETA),
        'xa_wo': nrm((L, d, d), d ** -0.5 * DEEPNORM_BETA),
        'ln2_g': 1.0 + nrm((L, d), 0.02),
        'ln2_b': nrm((L, d), 0.01),
        'router_w': nrm((L, d, N_EXPERTS), d ** -0.5),
        'router_bias': nrm((L, N_EXPERTS), 0.01),
        'exp_w_gate': nrm((L, N_EXPERTS, d, EXPERT_FF), d ** -0.5),
        'exp_w_up': nrm((L, N_EXPERTS, d, EXPERT_FF), d ** -0.5),
        'exp_w_down': nrm((L, N_EXPERTS, EXPERT_FF, d), EXPERT_FF ** -0.5 * DEEPNORM_BETA),
        'sh_w_gate': nrm((L, d, SHARED_FF), d ** -0.5),
        'sh_w_up': nrm((L, d, SHARED_FF), d ** -0.5),
        'sh_w_down': nrm((L, SHARED_FF, d), SHARED_FF ** -0.5 * DEEPNORM_BETA),
        'ln3_g': 1.0 + nrm((L, d), 0.02),
        'ln3_b': nrm((L, d), 0.01),
    }


def reference(x_prompt, x_sample, mem_prompt, state_dn, state_dn_conv, state_ssm, state_ssm_conv,
              cache_mem_k, cache_mem_v, w_in, dn_conv_w, dn_a_log, dn_dt_bias, dn_norm_w, dn_w_out,
              ssm_conv_w, ssm_conv_b, ssm_a_log, ssm_dt_bias, ssm_d, ssm_norm_w, ssm_w_out, w_o,
              ln1_g, ln1_b, xa_wq, xa_wk, xa_wv, xa_wo, ln2_g, ln2_b, router_w, router_bias,
              exp_w_gate, exp_w_up, exp_w_down, sh_w_gate, sh_w_up, sh_w_down, ln3_g, ln3_b):
    bp = x_prompt.shape[0]
    hp, hs = x_prompt, x_sample
    dn_p, dnc_p, ssm_p, ssmc_p, mk_p, mv_p = [], [], [], [], [], []
    dn_q, dnc_q, ssm_q, ssmc_q = [], [], [], []
    for i in range(DEPTH):
        mixer_w = (w_in[i], dn_conv_w[i], dn_a_log[i], dn_dt_bias[i], dn_norm_w[i], dn_w_out[i],
                   ssm_conv_w[i], ssm_conv_b[i], ssm_a_log[i], ssm_dt_bias[i], ssm_d[i], ssm_norm_w[i],
                   ssm_w_out[i], w_o[i])
        moe_w = (router_w[i], router_bias[i], exp_w_gate[i], exp_w_up[i], exp_w_down[i],
                 sh_w_gate[i], sh_w_up[i], sh_w_down[i])
        ln_w = (ln1_g[i], ln1_b[i], ln2_g[i], ln2_b[i], ln3_g[i], ln3_b[i])

        mk = (mem_prompt @ xa_wk[i]).reshape(bp, N_MEM, MEM_HEADS, MEM_HEAD_DIM)
        mv = (mem_prompt @ xa_wv[i]).reshape(bp, N_MEM, MEM_HEADS, MEM_HEAD_DIM)
        hp, s_dn, b_dn, s_ssm, b_ssm = decoder_layer(
            hp,
            jnp.zeros((bp, DN_V_HEADS, DN_HEAD_K, DN_HEAD_V), F32),
            jnp.zeros((bp, CONV_WIDTH - 1, DN_CONV_DIM), F32),
            jnp.zeros((bp, SSM_HEADS, SSM_HEAD_DIM, SSM_D_STATE), F32),
            jnp.zeros((bp, CONV_WIDTH - 1, SSM_CONV_DIM), F32),
            mk, mv, mixer_w, xa_wq[i], xa_wo[i], moe_w, ln_w)
        dn_p.append(s_dn); dnc_p.append(b_dn); ssm_p.append(s_ssm); ssmc_p.append(b_ssm)
        mk_p.append(mk); mv_p.append(mv)

        hs, s_dn, b_dn, s_ssm, b_ssm = decoder_layer(
            hs, state_dn[i], state_dn_conv[i], state_ssm[i], state_ssm_conv[i],
            cache_mem_k[i], cache_mem_v[i], mixer_w, xa_wq[i], xa_wo[i], moe_w, ln_w)
        dn_q.append(s_dn); dnc_q.append(b_dn); ssm_q.append(s_ssm); ssmc_q.append(b_ssm)

    return (hp.astype(x_prompt.dtype), hs.astype(x_sample.dtype),
            jnp.stack(dn_p), jnp.stack(dnc_p), jnp.stack(ssm_p), jnp.stack(ssmc_p),
            jnp.stack(mk_p), jnp.stack(mv_p),
            jnp.stack(dn_q), jnp.stack(dnc_q), jnp.stack(ssm_q), jnp.stack(ssmc_q))
```

```python
import functools

import jax
import jax.numpy as jnp
from jax import lax
from jax.experimental import pallas as pl
from jax.experimental.pallas import tpu as pltpu

F32 = jnp.float32
BF16 = jnp.bfloat16

CONV_WIDTH = 4
CHUNK = 64
DN_HEAD = 128
SSM_HEAD_DIM = 64
SSM_GROUPS = 8
SSM_D_STATE = 128
MEM_HEADS = 4
N_EXPERT_GROUPS = 8
TOPK_GROUPS = 4
TOP_K = 8
ROUTED_SCALE = 2.5
LN_EPS = 1e-5
RMS_EPS = 1e-6
L2_EPS = 1e-6
LANES = 128
DN_GROUP = 4
VMEM_LIMIT = 56 * 1024 * 1024


def _pick(n, cands):
    for c in cands:
        if n % c == 0:
            return c
    raise ValueError(f"no tile for {n} in {cands}")


def _params(*sem):
    return pltpu.CompilerParams(dimension_semantics=sem, vmem_limit_bytes=VMEM_LIMIT)


def _sigmoid(x):
    return 1.0 / (1.0 + jnp.exp(-x))


def _silu(x):
    return x * _sigmoid(x)


def _softplus(x):
    return jnp.maximum(x, 0.0) + jnp.log1p(jnp.exp(-jnp.abs(x)))


def _dot(a, b):
    return jnp.dot(a.astype(BF16), b.astype(BF16), preferred_element_type=F32)


def _dot_nt(a, b):
    return lax.dot_general(a.astype(BF16), b.astype(BF16), (((1,), (1,)), ((), ())),
                           preferred_element_type=F32)


def _dot_tn(a, b):
    return lax.dot_general(a.astype(BF16), b.astype(BF16), (((0,), (0,)), ((), ())),
                           preferred_element_type=F32)


def _split(x):
    hi = x.astype(BF16)
    lo = (x - hi.astype(F32)).astype(BF16)
    return hi, lo


def _dot_exact_rhs(a, b01):
    hi, lo = _split(a)
    b = b01.astype(BF16)
    return (jnp.dot(hi, b, preferred_element_type=F32) + jnp.dot(lo, b, preferred_element_type=F32))


def _dot_exact_lhs(a01, b):
    hi, lo = _split(b)
    a = a01.astype(BF16)
    return (jnp.dot(a, hi, preferred_element_type=F32) + jnp.dot(a, lo, preferred_element_type=F32))


def _tri(c, kind):
    r = lax.broadcasted_iota(jnp.int32, (c, c), 0)
    q = lax.broadcasted_iota(jnp.int32, (c, c), 1)
    if kind == "lower":
        return r >= q
    if kind == "strict":
        return r > q
    if kind == "upper":
        return r <= q
    raise ValueError(kind)


def _mm_kernel(x_ref, w_ref, o_ref):
    o_ref[...] = jnp.dot(x_ref[...], w_ref[...], preferred_element_type=F32).astype(o_ref.dtype)


def mm(x, w, out_dtype=F32):
    m, k = x.shape
    n = w.shape[1]
    tm = _pick(m, (640, 512, 256, 128, 64, 32, 16, 8))
    tn = _pick(n, (1024, 512, 256, 128))
    return pl.pallas_call(
        _mm_kernel,
        out_shape=jax.ShapeDtypeStruct((m, n), out_dtype),
        grid=(n // tn, m // tm),
        in_specs=[pl.BlockSpec((tm, k), lambda j, i: (i, 0)),
                  pl.BlockSpec((k, tn), lambda j, i: (0, j))],
        out_specs=pl.BlockSpec((tm, tn), lambda j, i: (i, j)),
        compiler_params=_params("parallel", "arbitrary"),
        name="mm",
    )(x, w)


def _ln_kernel(res_ref, y_ref, g_ref, b_ref, of_ref, ob_ref, *, alpha):
    x = alpha * res_ref[...] + y_ref[...]
    mu = jnp.mean(x, axis=-1, keepdims=True)
    xc = x - mu
    var = jnp.mean(xc * xc, axis=-1, keepdims=True)
    out = xc * lax.rsqrt(var + LN_EPS) * g_ref[...] + b_ref[...]
    of_ref[...] = out
    ob_ref[...] = out.astype(BF16)


def ln_res(res, y, g, b, alpha):
    t, d = res.shape
    tm = _pick(t, (320, 256, 128, 64, 32, 16, 8))
    row = pl.BlockSpec((tm, d), lambda i: (i, 0))
    vec = pl.BlockSpec((1, d), lambda i: (0, 0))
    return pl.pallas_call(
        functools.partial(_ln_kernel, alpha=alpha),
        out_shape=(jax.ShapeDtypeStruct((t, d), F32), jax.ShapeDtypeStruct((t, d), BF16)),
        grid=(t // tm,),
        in_specs=[row, row, vec, vec],
        out_specs=(row, row),
        compiler_params=_params("parallel"),
        name="ln_res",
    )(res, y, g.reshape(1, d), b.reshape(1, d))


def _merge_kernel(ga_ref, gb_ref, ya_ref, yb_ref, o_ref):
    o = _sigmoid(ga_ref[...]) * ya_ref[...] + _sigmoid(gb_ref[...]) * yb_ref[...]
    o_ref[...] = o.astype(o_ref.dtype)


def gate_merge(proj, gate_off, ya, yb):
    t, d = ya.shape
    tm = _pick(t, (320, 256, 128, 64, 32, 16, 8))
    tc = 512
    nb = d // tc
    off = gate_off // tc
    return pl.pallas_call(
        _merge_kernel,
        out_shape=jax.ShapeDtypeStruct((t, d), BF16),
        grid=(t // tm, nb),
        in_specs=[pl.BlockSpec((tm, tc), lambda i, j: (i, off + j)),
                  pl.BlockSpec((tm, tc), lambda i, j: (i, off + nb + j)),
                  pl.BlockSpec((tm, tc), lambda i, j: (i, j)),
                  pl.BlockSpec((tm, tc), lambda i, j: (i, j))],
        out_specs=pl.BlockSpec((tm, tc), lambda i, j: (i, j)),
        compiler_params=_params("parallel", "parallel"),
        name="gate_merge",
    )(proj, proj, ya, yb)


def _gates_kernel(p_ref, bias_ref, alog_ref, o1_ref, o2_ref, *, n_beta, n_g):
    p = p_ref[...]
    lane = lax.broadcasted_iota(jnp.int32, p.shape, 1)
    sp = _softplus(p + bias_ref[...])
    neg_a = -jnp.exp(alog_ref[...])
    o1_ref[...] = jnp.where(lane < n_beta, _sigmoid(p), jnp.where(lane < n_beta + n_g, neg_a * sp, sp))
    o2_ref[...] = neg_a * sp


def gates(p_tail, bias_row, alog_row, n_beta, n_g):
    t, w = p_tail.shape
    tm = _pick(t, (640, 512, 256, 128, 64, 32, 16, 8))
    row = pl.BlockSpec((tm, w), lambda i: (i, 0))
    vec = pl.BlockSpec((1, w), lambda i: (0, 0))
    return pl.pallas_call(
        functools.partial(_gates_kernel, n_beta=n_beta, n_g=n_g),
        out_shape=(jax.ShapeDtypeStruct((t, w), F32), jax.ShapeDtypeStruct((t, w), F32)),
        grid=(t // tm,),
        in_specs=[row, vec, vec],
        out_specs=(row, row),
        compiler_params=_params("parallel"),
        name="gates",
    )(p_tail, bias_row, alog_row)


def _head_l2norm(y, scale_row):
    parts = []
    for g in range(y.shape[1] // DN_HEAD):
        seg = y[:, g * DN_HEAD:(g + 1) * DN_HEAD]
        ss = jnp.sum(seg * seg, axis=-1, keepdims=True)
        parts.append(seg * lax.rsqrt(ss + L2_EPS))
    return jnp.concatenate(parts, axis=1) * scale_row


def _conv_prompt_kernel(x_ref, w_ref, b_ref, s_ref, o_ref, *, norm):
    x = x_ref[...]
    row = lax.broadcasted_iota(jnp.int32, x.shape, 0)
    acc = x * w_ref[CONV_WIDTH - 1:CONV_WIDTH, :]
    for j in range(1, CONV_WIDTH):
        xs = jnp.where(row >= j, pltpu.roll(x, j, axis=0), 0.0)
        acc = acc + xs * w_ref[CONV_WIDTH - 1 - j:CONV_WIDTH - j, :]
    y = _silu(acc + b_ref[...])
    if norm:
        y = _head_l2norm(y, s_ref[...])
    o_ref[...] = y


def conv_prompt(proj, col_off, width, seq, batch, w, bias, scale, norm):
    tc = 256
    off = col_off // tc
    vec = lambda r: pl.BlockSpec((r, tc), lambda b, j: (0, j))
    return pl.pallas_call(
        functools.partial(_conv_prompt_kernel, norm=norm),
        out_shape=jax.ShapeDtypeStruct((batch * seq, width), F32),
        grid=(batch, width // tc),
        in_specs=[pl.BlockSpec((seq, tc), lambda b, j: (b, off + j)), vec(CONV_WIDTH), vec(1), vec(1)],
        out_specs=pl.BlockSpec((seq, tc), lambda b, j: (b, j)),
        compiler_params=_params("parallel", "parallel"),
        name="conv_prompt",
    )(proj, w, bias, scale)


def _conv_step_kernel(x_ref, buf_ref, w_ref, b_ref, s_ref, o_ref, *, norm):
    acc = x_ref[...] * w_ref[CONV_WIDTH - 1:CONV_WIDTH, :]
    for j in range(CONV_WIDTH - 1):
        acc = acc + buf_ref[j] * w_ref[j:j + 1, :]
    y = _silu(acc + b_ref[...])
    if norm:
        y = _head_l2norm(y, s_ref[...])
    o_ref[...] = y


def conv_step(proj, row_off, nrows, col_off, width, buf_t, w, bias, scale, norm):
    tc = 512
    off = col_off // tc
    rb = row_off // nrows
    vec = lambda r: pl.BlockSpec((r, tc), lambda j: (0, j))
    return pl.pallas_call(
        functools.partial(_conv_step_kernel, norm=norm),
        out_shape=jax.ShapeDtypeStruct((nrows, width), F32),
        grid=(width // tc,),
        in_specs=[pl.BlockSpec((nrows, tc), lambda j: (rb, off + j)),
                  pl.BlockSpec((CONV_WIDTH - 1, nrows, tc), lambda j: (0, 0, j)),
                  vec(CONV_WIDTH), vec(1), vec(1)],
        out_specs=pl.BlockSpec((nrows, tc), lambda j: (0, j)),
        compiler_params=_params("parallel"),
        name="conv_step",
    )(proj, buf_t, w, bias, scale)


def _gated_rms(o, nw_row, z):
    ms = jnp.mean(o * o, axis=-1, keepdims=True)
    return o * lax.rsqrt(ms + RMS_EPS) * nw_row * _silu(z)


def _delta_prompt_kernel(q_ref, k_ref, v_ref, z_ref, gc_ref, gr_ref, nw_ref, o_ref, s_ref):
    c = q_ref.shape[0]
    G = DN_GROUP
    n = pl.program_id(2)

    @pl.when(n == 0)
    def _():
        s_ref[...] = jnp.zeros_like(s_ref)

    low = _tri(c, "lower")
    strict = _tri(c, "strict")
    up = _tri(c, "upper")
    eye = jnp.where(low & up, 1.0, 0.0).astype(F32)

    gcol = gc_ref[...]
    cum_col = _dot_exact_lhs(jnp.where(low, 1.0, 0.0), gcol)
    cum_row = _dot_exact_rhs(gr_ref[0, 0, 0], jnp.where(up, 1.0, 0.0))

    for gi in range(G):
        kh = gi // 2
        q = q_ref[:, kh * DN_HEAD:(kh + 1) * DN_HEAD]
        k = k_ref[:, kh * DN_HEAD:(kh + 1) * DN_HEAD]
        v = v_ref[:, gi * DN_HEAD:(gi + 1) * DN_HEAD]
        beta = gcol[:, gi:gi + 1]
        gc = cum_col[:, G + gi:G + gi + 1]
        gr = cum_row[gi:gi + 1, :]
        decay = jnp.where(low, jnp.exp(jnp.where(low, gc - gr, 0.0)), 0.0)
        egc = jnp.exp(gc)
        kb = k * beta
        a = jnp.where(strict, _dot_nt(kb, k) * decay, 0.0)
        pw = -a
        inv = eye + pw
        m = 2
        while m < c:
            pw = _dot(pw, pw)
            inv = inv + _dot(inv, pw)
            m *= 2
        rhs = jnp.concatenate([v * beta, kb * egc], axis=1)
        sol = _dot(inv, rhs)
        u, w = sol[:, :DN_HEAD], sol[:, DN_HEAD:]
        s = s_ref[0, gi]
        v_new = u - _dot(w, s)
        attn = jnp.where(low, _dot_nt(q, k) * decay, 0.0)
        o = _dot(q * egc, s) + _dot(attn, v_new)
        g_last = gc[c - 1:c, :]
        s_ref[0, gi] = s * jnp.exp(g_last) + _dot_tn(k * jnp.exp(g_last - gc), v_new)
        z = z_ref[:, gi * DN_HEAD:(gi + 1) * DN_HEAD]
        o_ref[:, gi * DN_HEAD:(gi + 1) * DN_HEAD] = _gated_rms(o, nw_ref[...], z).astype(o_ref.dtype)


def delta_prompt(qkv, proj, z_off, gcol, grow, norm_w, batch, seq, n_vheads):
    c = CHUNK
    nc = seq // c
    G = DN_GROUP
    hg = n_vheads // G
    qw = (G // 2) * DN_HEAD
    vw = G * DN_HEAD
    n_kheads = n_vheads // 2
    k_off = n_kheads * DN_HEAD // qw
    v_off = 2 * n_kheads * DN_HEAD // vw
    zb = z_off // vw
    rowblk = lambda b, h, n: b * nc + n
    return pl.pallas_call(
        _delta_prompt_kernel,
        out_shape=(jax.ShapeDtypeStruct((batch * seq, n_vheads * DN_HEAD), BF16),
                   jax.ShapeDtypeStruct((batch, n_vheads, DN_HEAD, DN_HEAD), F32)),
        grid=(batch, hg, nc),
        in_specs=[pl.BlockSpec((c, qw), lambda b, h, n: (rowblk(b, h, n), h)),
                  pl.BlockSpec((c, qw), lambda b, h, n: (rowblk(b, h, n), k_off + h)),
                  pl.BlockSpec((c, vw), lambda b, h, n: (rowblk(b, h, n), v_off + h)),
                  pl.BlockSpec((c, vw), lambda b, h, n: (rowblk(b, h, n), zb + h)),
                  pl.BlockSpec((c, LANES), lambda b, h, n: (rowblk(b, h, n), h)),
                  pl.BlockSpec((1, 1, 1, 8, c), lambda b, h, n: (b, n, h, 0, 0)),
                  pl.BlockSpec((1, DN_HEAD), lambda b, h, n: (0, 0))],
        out_specs=(pl.BlockSpec((c, vw), lambda b, h, n: (rowblk(b, h, n), h)),
                   pl.BlockSpec((1, G, DN_HEAD, DN_HEAD), lambda b, h, n: (b, h, 0, 0))),
        compiler_params=_params("parallel", "parallel", "arbitrary"),
        name="delta_prompt",
    )(qkv, qkv, qkv, proj, gcol, grow, norm_w)


def _delta_step_kernel(qt_ref, kt_ref, v_ref, z_ref, gt_ref, nw_ref, s_ref, o_ref, so_ref, *, n_vheads):
    qt = qt_ref[0]
    kt = kt_ref[0]
    gt = gt_ref[0]
    for h in range(n_vheads):
        kh = h // 2
        qc = qt[:, kh:kh + 1]
        kc = kt[:, kh:kh + 1]
        v = v_ref[0, :, h * DN_HEAD:(h + 1) * DN_HEAD]
        beta = gt[:, h:h + 1]
        eg = jnp.exp(gt[:, n_vheads + h:n_vheads + h + 1])
        s = s_ref[0, h]
        ks = jnp.sum(kc * s, axis=0, keepdims=True)
        qs = jnp.sum(qc * s, axis=0, keepdims=True)
        qk = jnp.sum(qc * kc, axis=0, keepdims=True)
        v_new = beta * v - (beta * eg) * ks
        o = eg * qs + qk * v_new
        so_ref[0, h] = s * eg + kc * v_new
        z = z_ref[0, :, h * DN_HEAD:(h + 1) * DN_HEAD]
        o_ref[0, :, h * DN_HEAD:(h + 1) * DN_HEAD] = _gated_rms(o, nw_ref[...], z).astype(o_ref.dtype)


def delta_step(qt, kt, v3, z3, g3, norm_w, state):
    nb, n_vheads = state.shape[0], state.shape[1]
    hk = qt.shape[2]
    vw = n_vheads * DN_HEAD
    blk3 = lambda a, b_: pl.BlockSpec((1, a, b_), lambda i: (i, 0, 0))
    st = pl.BlockSpec((1, n_vheads, DN_HEAD, DN_HEAD), lambda i: (i, 0, 0, 0))
    return pl.pallas_call(
        functools.partial(_delta_step_kernel, n_vheads=n_vheads),
        out_shape=(jax.ShapeDtypeStruct((nb, 1, vw), BF16), jax.ShapeDtypeStruct(state.shape, F32)),
        grid=(nb,),
        in_specs=[blk3(DN_HEAD, hk), blk3(DN_HEAD, hk), blk3(1, vw), blk3(1, vw), blk3(1, LANES),
                  pl.BlockSpec((1, DN_HEAD), lambda i: (0, 0)), st],
        out_specs=(blk3(1, vw), st),
        compiler_params=_params("parallel"),
        name="delta_step",
    )(qt, kt, v3, z3, g3, norm_w, state)


def _ssd_finish(y, xs, z, d_row, nw_row):
    y = (y + d_row * xs) * _silu(z)
    ms = jnp.mean(y * y, axis=-1, keepdims=True)
    return y * lax.rsqrt(ms + RMS_EPS) * nw_row


def _ssd_prompt_kernel(x_ref, b_ref, c_ref, z_ref, gc_ref, gr_ref, d_ref, nw_ref, o_ref, s_ref):
    c = x_ref.shape[0]
    R = s_ref.shape[1]
    P = SSM_HEAD_DIM
    n = pl.program_id(2)

    @pl.when(n == 0)
    def _():
        s_ref[...] = jnp.zeros_like(s_ref)

    low = _tri(c, "lower")
    up = _tri(c, "upper")
    gcol = gc_ref[...]
    cum_col = _dot_exact_lhs(jnp.where(low, 1.0, 0.0), gcol)
    grow = gr_ref[0, 0, 0]
    cum_row = _dot_exact_rhs(grow, jnp.where(up, 1.0, 0.0))

    xs = x_ref[...]
    bm = b_ref[...]
    cm = c_ref[...]
    cb = _dot_nt(cm, bm)

    lane = lax.broadcasted_iota(jnp.int32, gcol.shape, 1)
    is_cum = (lane >= R) & (lane < 2 * R)
    ecum = jnp.where(is_cum, jnp.exp(jnp.where(is_cum, cum_col, 0.0)), 0.0)
    er = lax.broadcasted_iota(jnp.int32, (LANES, R * P), 0)
    ec = lax.broadcasted_iota(jnp.int32, (LANES, R * P), 1)
    expand = jnp.where(er - R == ec // P, 1.0, 0.0)
    ecum_x = _dot_exact_rhs(ecum, expand)

    s_all = s_ref[0].reshape(R * P, SSM_D_STATE)
    y = _dot_nt(cm, s_all) * ecum_x

    plane = lax.broadcasted_iota(jnp.int32, (c, 2 * P), 1)
    last = cum_col[c - 1:c, :]
    y1 = []
    for pr in range(R // 2):
        xp = xs[:, pr * 2 * P:(pr + 1) * 2 * P]
        acc = None
        for hh in range(2):
            r = 2 * pr + hh
            cc = cum_col[:, R + r:R + r + 1]
            cr = cum_row[R + r:R + r + 1, :]
            seg = jnp.where(low, jnp.exp(jnp.where(low, cc - cr, 0.0)), 0.0)
            mh = cb * seg * grow[r:r + 1, :]
            keep = (plane >= hh * P) & (plane < (hh + 1) * P)
            t = _dot(mh, jnp.where(keep, xp, 0.0))
            acc = t if acc is None else acc + t
            lr = last[:, R + r:R + r + 1]
            wcol = gcol[:, r:r + 1] * jnp.exp(lr - cc)
            upd = _dot_tn(xp, bm * wcol)
            s_ref[0, r] = s_ref[0, r] * jnp.exp(lr) + upd[hh * P:(hh + 1) * P, :]
        y1.append(acc)
    y = y + jnp.concatenate(y1, axis=1)
    o_ref[...] = _ssd_finish(y, xs, z_ref[...], d_ref[...], nw_ref[...]).astype(o_ref.dtype)


def ssd_prompt(xbc, proj, z_off, gcol, grow, d_exp, norm_w, batch, seq, n_heads):
    c = CHUNK
    nc = seq // c
    R = n_heads // SSM_GROUPS
    gw = R * SSM_HEAD_DIM
    inner = n_heads * SSM_HEAD_DIM
    b_off = inner // SSM_D_STATE
    c_off = b_off + SSM_GROUPS
    zb = z_off // gw
    rowblk = lambda b, g, n: b * nc + n
    return pl.pallas_call(
        _ssd_prompt_kernel,
        out_shape=(jax.ShapeDtypeStruct((batch * seq, inner), BF16),
                   jax.ShapeDtypeStruct((batch, n_heads, SSM_HEAD_DIM, SSM_D_STATE), F32)),
        grid=(batch, SSM_GROUPS, nc),
        in_specs=[pl.BlockSpec((c, gw), lambda b, g, n: (rowblk(b, g, n), g)),
                  pl.BlockSpec((c, SSM_D_STATE), lambda b, g, n: (rowblk(b, g, n), b_off + g)),
                  pl.BlockSpec((c, SSM_D_STATE), lambda b, g, n: (rowblk(b, g, n), c_off + g)),
                  pl.BlockSpec((c, gw), lambda b, g, n: (rowblk(b, g, n), zb + g)),
                  pl.BlockSpec((c, LANES), lambda b, g, n: (rowblk(b, g, n), g)),
                  pl.BlockSpec((1, 1, 1, 2 * R, c), lambda b, g, n: (b, n, g, 0, 0)),
                  pl.BlockSpec((1, gw), lambda b, g, n: (0, g)),
                  pl.BlockSpec((1, gw), lambda b, g, n: (0, g))],
        out_specs=(pl.BlockSpec((c, gw), lambda b, g, n: (rowblk(b, g, n), g)),
                   pl.BlockSpec((1, R, SSM_HEAD_DIM, SSM_D_STATE), lambda b, g, n: (b, g, 0, 0))),
        compiler_params=_params("parallel", "parallel", "arbitrary"),
        name="ssd_prompt",
    )(xbc, xbc, xbc, proj, gcol, grow, d_exp, norm_w)


def _ssd_step_kernel(xt_ref, xr_ref, b_ref, c_ref, z_ref, g1_ref, g2_ref, d_ref, nw_ref, s_ref,
                     o_ref, so_ref, *, n_heads):
    R = n_heads // SSM_GROUPS
    P = SSM_HEAD_DIM
    gw = R * P
    xt = xt_ref[0]
    g1 = g1_ref[0]
    g2 = g2_ref[0]
    off = LANES - n_heads
    for g in range(SSM_GROUPS):
        brow = b_ref[0, g:g + 1, :]
        for r in range(R):
            h = g * R + r
            dt = g1[:, off + h:off + h + 1]
            eda = jnp.exp(g2[:, off + h:off + h + 1])
            so_ref[0, h] = s_ref[0, h] * eda + (xt[:, h:h + 1] * dt) * brow
        s_new = so_ref[0, g * R:(g + 1) * R].reshape(gw, SSM_D_STATE)
        c8 = jnp.broadcast_to(c_ref[0, g:g + 1, :], (8, SSM_D_STATE))
        y = _dot_nt(c8, s_new)[0:1, :]
        sl = slice(g * gw, (g + 1) * gw)
        o_ref[0, :, sl] = _ssd_finish(y, xr_ref[0, :, sl], z_ref[0, :, sl], d_ref[:, sl],
                                      nw_ref[:, sl]).astype(o_ref.dtype)


def ssd_step(xt, x3, b3, c3, z3, g1, g2, d_exp, norm_w, state):
    nb, n_heads = state.shape[0], state.shape[1]
    inner = n_heads * SSM_HEAD_DIM
    blk3 = lambda a, b_: pl.BlockSpec((1, a, b_), lambda i: (i, 0, 0))
    st = pl.BlockSpec((1, n_heads, SSM_HEAD_DIM, SSM_D_STATE), lambda i: (i, 0, 0, 0))
    vec = pl.BlockSpec((1, inner), lambda i: (0, 0))
    return pl.pallas_call(
        functools.partial(_ssd_step_kernel, n_heads=n_heads),
        out_shape=(jax.ShapeDtypeStruct((nb, 1, inner), BF16), jax.ShapeDtypeStruct(state.shape, F32)),
        grid=(nb,),
        in_specs=[blk3(SSM_HEAD_DIM, n_heads), blk3(1, inner), blk3(SSM_GROUPS, SSM_D_STATE),
                  blk3(SSM_GROUPS, SSM_D_STATE), blk3(1, inner), blk3(1, LANES), blk3(1, LANES),
                  vec, vec, st],
        out_specs=(blk3(1, inner), st),
        compiler_params=_params("parallel"),
        name="ssd_step",
    )(xt, x3, b3, c3, z3, g1, g2, d_exp, norm_w, state)


def _attn_kernel(q_ref, k_ref, v_ref, o_ref):
    d = q_ref.shape[2]
    hd = d // MEM_HEADS
    scale = hd ** -0.5
    for h in range(MEM_HEADS):
        sl = slice(h * hd, (h + 1) * hd)
        s = _dot_nt(q_ref[0, :, sl], k_ref[0, :, sl]) * scale
        s = s - jnp.max(s, axis=-1, keepdims=True)
        p = jnp.exp(s)
        p = p / jnp.sum(p, axis=-1, keepdims=True)
        o_ref[0, :, sl] = _dot(p, v_ref[0, :, sl]).astype(o_ref.dtype)


def attention(q, k, v):
    b, l, d = q.shape
    m = k.shape[1]
    tl = _pick(l, (512, 256, 128, 64, 32, 16, 8, 1))
    return pl.pallas_call(
        _attn_kernel,
        out_shape=jax.ShapeDtypeStruct((b, l, d), BF16),
        grid=(b, l // tl),
        in_specs=[pl.BlockSpec((1, tl, d), lambda i, j: (i, j, 0)),
                  pl.BlockSpec((1, m, d), lambda i, j: (i, 0, 0)),
                  pl.BlockSpec((1, m, d), lambda i, j: (i, 0, 0))],
        out_specs=pl.BlockSpec((1, tl, d), lambda i, j: (i, j, 0)),
        compiler_params=_params("parallel", "arbitrary"),
        name="attention",
    )(q, k, v)


def _router_kernel(t_ref, w_ref, b_ref, o_ref):
    t = t_ref[...]
    w = w_ref[...]
    th, tl = _split(t)
    wh, wl = _split(w)
    nt = lambda a, b_: lax.dot_general(a, b_, (((1,), (1,)), ((), ())), preferred_element_type=F32)
    logits = nt(wh, th) + (nt(wh, tl) + nt(wl, th))
    scores = _sigmoid(logits)
    sel = scores + b_ref[...]
    ne, tm = sel.shape
    epg = ne // N_EXPERT_GROUPS
    neg = -jnp.inf

    s3 = sel.reshape(N_EXPERT_GROUPS, epg, tm)
    i3 = lax.broadcasted_iota(jnp.int32, s3.shape, 1)
    m1 = jnp.max(s3, axis=1, keepdims=True)
    a1 = jnp.min(jnp.where(s3 == m1, i3, epg), axis=1, keepdims=True)
    m2 = jnp.max(jnp.where(i3 == a1, neg, s3), axis=1, keepdims=True)
    grp = (m1 + m2).reshape(N_EXPERT_GROUPS, tm)

    gi = lax.broadcasted_iota(jnp.int32, grp.shape, 0)
    gsel = jnp.zeros(grp.shape, jnp.bool_)
    for _ in range(TOPK_GROUPS):
        mx = jnp.max(grp, axis=0, keepdims=True)
        am = jnp.min(jnp.where(grp == mx, gi, N_EXPERT_GROUPS), axis=0, keepdims=True)
        hit = gi == am
        gsel = gsel | hit
        grp = jnp.where(hit, neg, grp)

    gmask = jnp.broadcast_to(gsel.reshape(N_EXPERT_GROUPS, 1, tm), s3.shape).reshape(ne, tm)
    cur = jnp.where(gmask, sel, neg)
    ei = lax.broadcasted_iota(jnp.int32, cur.shape, 0)
    chosen = jnp.zeros(cur.shape, jnp.bool_)
    for _ in range(TOP_K):
        mx = jnp.max(cur, axis=0, keepdims=True)
        am = jnp.min(jnp.where(cur == mx, ei, ne), axis=0, keepdims=True)
        hit = ei == am
        chosen = chosen | hit
        cur = jnp.where(hit, neg, cur)

    wts = jnp.where(chosen, scores, 0.0)
    o_ref[...] = wts / jnp.sum(wts, axis=0, keepdims=True) * ROUTED_SCALE


def router(t, w_t, bias_col):
    tt, d = t.shape
    ne = w_t.shape[0]
    tm = _pick(tt, (640, 512, 256, 128))
    return pl.pallas_call(
        _router_kernel,
        out_shape=jax.ShapeDtypeStruct((ne, tt), F32),
        grid=(tt // tm,),
        in_specs=[pl.BlockSpec((tm, d), lambda i: (i, 0)),
                  pl.BlockSpec((ne, d), lambda i: (0, 0)),
                  pl.BlockSpec((ne, 1), lambda i: (0, 0))],
        out_specs=pl.BlockSpec((ne, tm), lambda i: (0, i)),
        compiler_params=_params("parallel"),
        name="router",
    )(t, w_t, bias_col)


def _moe_kernel(t_ref, g_ref, wg_ref, wu_ref, wd_ref, o_ref):
    e = pl.program_id(1)

    @pl.when(e == 0)
    def _():
        o_ref[...] = jnp.zeros_like(o_ref)

    t = t_ref[...]
    gate = g_ref[...]
    lane = lax.broadcasted_iota(jnp.int32, gate.shape, 1)
    gcol = jnp.sum(jnp.where(lane == e, gate, 0.0), axis=1, keepdims=True)
    hid = _silu(jnp.dot(t, wg_ref[0], preferred_element_type=F32)) * jnp.dot(
        t, wu_ref[0], preferred_element_type=F32)
    o_ref[...] += _dot(hid * gcol, wd_ref[0])


def moe_dense(t, gate, wg, wu, wd):
    tt, d = t.shape
    ne, _, ff = wg.shape
    tm = _pick(tt, (640, 512, 256, 128))
    return pl.pallas_call(
        _moe_kernel,
        out_shape=jax.ShapeDtypeStruct((tt, d), F32),
        grid=(tt // tm, ne),
        in_specs=[pl.BlockSpec((tm, d), lambda i, e: (i, 0)),
                  pl.BlockSpec((tm, LANES), lambda i, e: (i, 0)),
                  pl.BlockSpec((1, d, ff), lambda i, e: (e, 0, 0)),
                  pl.BlockSpec((1, d, ff), lambda i, e: (e, 0, 0)),
                  pl.BlockSpec((1, ff, d), lambda i, e: (e, 0, 0))],
        out_specs=pl.BlockSpec((tm, d), lambda i, e: (i, 0)),
        compiler_params=_params("parallel", "arbitrary"),
        name="moe_dense",
    )(t, gate, wg, wu, wd)


def _pad_lanes(a, width=LANES):
    return jnp.pad(a, [(0, 0)] * (a.ndim - 1) + [(0, width - a.shape[-1])])


def _group_cols(parts, n_groups):
    t = parts[0].shape[0]
    per = parts[0].shape[1] // n_groups
    cat = jnp.concatenate([p.reshape(t, n_groups, per) for p in parts], axis=2)
    return _pad_lanes(cat).reshape(t, n_groups * LANES)


def _group_rows(parts, batch, nc, c, n_groups, rows):
    per = parts[0].shape[1] // n_groups
    cat = jnp.concatenate([p.reshape(batch, nc, c, n_groups, per) for p in parts], axis=4)
    cat = jnp.transpose(cat, (0, 1, 3, 4, 2))
    return jnp.pad(cat, [(0, 0)] * 3 + [(0, rows - cat.shape[3]), (0, 0)])


def kernel(x_prompt, x_sample, mem_prompt, state_dn, state_dn_conv, state_ssm, state_ssm_conv, cache_mem_k, cache_mem_v, w_in, dn_conv_w, dn_a_log, dn_dt_bias, dn_norm_w, dn_w_out, ssm_conv_w, ssm_conv_b, ssm_a_log, ssm_dt_bias, ssm_d, ssm_norm_w, ssm_w_out, w_o, ln1_g, ln1_b, xa_wq, xa_wk, xa_wv, xa_wo, ln2_g, ln2_b, router_w, router_bias, exp_w_gate, exp_w_up, exp_w_down, sh_w_gate, sh_w_up, sh_w_down, ln3_g, ln3_b):
    bp, seq, d = x_prompt.shape
    bs = x_sample.shape[0]
    depth = w_in.shape[0]
    tp = bp * seq
    n_mem = mem_prompt.shape[1]
    hv = state_dn.shape[2]
    hk = hv // 2
    qk_w = hk * DN_HEAD
    v_w = hv * DN_HEAD
    dn_conv = 2 * qk_w + v_w
    hs = state_ssm.shape[2]
    inner = hs * SSM_HEAD_DIM
    bc_w = SSM_GROUPS * SSM_D_STATE
    ssm_conv = inner + 2 * bc_w
    ne = router_w.shape[2]
    alpha = (2 * depth) ** 0.25
    nc = seq // CHUNK
    R = hs // SSM_GROUPS

    o_qkv, o_z, o_b, o_a = 0, dn_conv, dn_conv + v_w, dn_conv + v_w + hv
    o_xbc = o_a + hv
    o_sz = o_xbc + ssm_conv
    o_dt = o_sz + inner
    o_gate = o_dt + hs
    p_qkv, p_z = 0, dn_conv
    p_xbc = p_z + v_w
    p_sz = p_xbc + ssm_conv
    p_gate = p_sz + inner

    h_f32 = jnp.concatenate([x_prompt.reshape(tp, d), x_sample.reshape(bs, d)], axis=0)
    h_bf = h_f32.astype(BF16)
    mem_bf = mem_prompt.reshape(bp * n_mem, d).astype(BF16)

    outs = {k: [] for k in ("dn_p", "dnc_p", "ssm_p", "ssmc_p", "mk_p", "mv_p", "dn_s", "dnc_s", "ssm_s", "ssmc_s")}

    for i in range(depth):
        wi = w_in[i]
        w_main = jnp.concatenate([wi[:, o_qkv:o_b], wi[:, o_xbc:o_dt], wi[:, o_gate:]], axis=1).astype(BF16)
        w_tail = jnp.concatenate([wi[:, o_b:o_xbc], wi[:, o_dt:o_gate]], axis=1)
        n_tail = w_tail.shape[1]
        w_tail = _pad_lanes(w_tail).astype(BF16)

        proj = mm(h_bf, w_main)
        p_tail = mm(h_bf, w_tail)

        zeros_hv = jnp.zeros((hv,), F32)
        bias_row = _pad_lanes(jnp.concatenate([zeros_hv, dn_dt_bias[i], ssm_dt_bias[i]]).reshape(1, n_tail))
        alog_row = _pad_lanes(jnp.concatenate([zeros_hv, dn_a_log[i], ssm_a_log[i]]).reshape(1, n_tail))
        g1, g2 = gates(p_tail, bias_row, alog_row, hv, hv)
        beta, glog = g1[:, :hv], g1[:, hv:2 * hv]
        dt, da = g1[:, 2 * hv:2 * hv + hs], g2[:, 2 * hv:2 * hv + hs]

        dn_scale = jnp.concatenate([jnp.full((qk_w,), DN_HEAD ** -0.5, F32), jnp.ones((qk_w,), F32)]).reshape(1, -1)
        dn_zero_b = jnp.zeros((1, dn_conv), F32)
        cw = dn_conv_w[i]
        qk_p = conv_prompt(proj, p_qkv, 2 * qk_w, seq, bp, cw[:, :2 * qk_w], dn_zero_b[:, :2 * qk_w], dn_scale, True)
        v_p = conv_prompt(proj, p_qkv + 2 * qk_w, v_w, seq, bp, cw[:, 2 * qk_w:], dn_zero_b[:, 2 * qk_w:],
                          dn_zero_b[:, 2 * qk_w:], False)
        qkv_p = jnp.concatenate([qk_p, v_p], axis=1)
        scw = ssm_conv_w[i]
        scb = ssm_conv_b[i].reshape(1, ssm_conv)
        xbc_p = conv_prompt(proj, p_xbc, ssm_conv, seq, bp, scw, scb, scb, False)

        dn_buf_t = jnp.transpose(state_dn_conv[i], (1, 0, 2))
        qk_s = conv_step(proj, tp, bs, p_qkv, 2 * qk_w, dn_buf_t[:, :, :2 * qk_w], cw[:, :2 * qk_w],
                         dn_zero_b[:, :2 * qk_w], dn_scale, True)
        v_s = conv_step(proj, tp, bs, p_qkv + 2 * qk_w, v_w, dn_buf_t[:, :, 2 * qk_w:], cw[:, 2 * qk_w:],
                        dn_zero_b[:, 2 * qk_w:], dn_zero_b[:, 2 * qk_w:], False)
        ssm_buf_t = jnp.transpose(state_ssm_conv[i], (1, 0, 2))
        xbc_s = conv_step(proj, tp, bs, p_xbc, ssm_conv, ssm_buf_t, scw, scb, scb, False)

        outs["dnc_p"].append(proj[:tp, p_qkv:p_qkv + dn_conv].reshape(bp, seq, dn_conv)[:, seq - (CONV_WIDTH - 1):])
        outs["ssmc_p"].append(proj[:tp, p_xbc:p_xbc + ssm_conv].reshape(bp, seq, ssm_conv)[:, seq - (CONV_WIDTH - 1):])
        outs["dnc_s"].append(jnp.concatenate([state_dn_conv[i][:, 1:], proj[tp:, None, p_qkv:p_qkv + dn_conv]], axis=1))
        outs["ssmc_s"].append(jnp.concatenate([state_ssm_conv[i][:, 1:], proj[tp:, None, p_xbc:p_xbc + ssm_conv]], axis=1))

        nw_dn = dn_norm_w[i].reshape(1, DN_HEAD)
        hg = hv // DN_GROUP
        dn_gcol = _group_cols([beta[:tp], glog[:tp]], hg)
        dn_grow = _group_rows([glog[:tp]], bp, nc, CHUNK, hg, 8)
        o_dn_p, s_dn_p = delta_prompt(qkv_p, proj, p_z, dn_gcol, dn_grow, nw_dn, bp, seq, hv)
        qt = jnp.transpose(qk_s[:, :qk_w].reshape(bs, hk, DN_HEAD), (0, 2, 1))
        kt = jnp.transpose(qk_s[:, qk_w:].reshape(bs, hk, DN_HEAD), (0, 2, 1))
        z_dn_s = proj[tp:, p_z:p_z + v_w].reshape(bs, 1, v_w)
        o_dn_s, s_dn_s = delta_step(qt, kt, v_s.reshape(bs, 1, v_w), z_dn_s, g1[tp:].reshape(bs, 1, LANES),
                                    nw_dn, state_dn[i])
        o_dn = jnp.concatenate([o_dn_p, o_dn_s.reshape(bs, v_w)], axis=0)
        outs["dn_p"].append(s_dn_p)
        outs["dn_s"].append(s_dn_s)

        d_exp = jnp.repeat(ssm_d[i], SSM_HEAD_DIM).reshape(1, inner)
        nw_ssm = ssm_norm_w[i].reshape(1, inner)
        ssm_gcol = _group_cols([dt[:tp], da[:tp]], SSM_GROUPS)
        ssm_grow = _group_rows([dt[:tp], da[:tp]], bp, nc, CHUNK, SSM_GROUPS, 2 * R)
        y_p, s_ssm_p = ssd_prompt(xbc_p, proj, p_sz, ssm_gcol, ssm_grow, d_exp, nw_ssm, bp, seq, hs)
        xs_s = xbc_s[:, :inner]
        xt_s = jnp.transpose(xs_s.reshape(bs, hs, SSM_HEAD_DIM), (0, 2, 1))
        b_s = xbc_s[:, inner:inner + bc_w].reshape(bs, SSM_GROUPS, SSM_D_STATE)
        c_s = xbc_s[:, inner + bc_w:].reshape(bs, SSM_GROUPS, SSM_D_STATE)
        z_ssm_s = proj[tp:, p_sz:p_sz + inner].reshape(bs, 1, inner)
        y_s, s_ssm_s = ssd_step(xt_s, xs_s.reshape(bs, 1, inner), b_s, c_s, z_ssm_s,
                                g1[tp:].reshape(bs, 1, LANES), g2[tp:].reshape(bs, 1, LANES),
                                d_exp, nw_ssm, state_ssm[i])
        y_ssm = jnp.concatenate([y_p, y_s.reshape(bs, inner)], axis=0)
        outs["ssm_p"].append(s_ssm_p)
        outs["ssm_s"].append(s_ssm_s)

        y_a = mm(o_dn, dn_w_out[i].astype(BF16))
        y_b = mm(y_ssm, ssm_w_out[i].astype(BF16))
        merged = gate_merge(proj, p_gate, y_a, y_b)
        mix = mm(merged, w_o[i].astype(BF16))
        h_f32, h_bf = ln_res(h_f32, mix, ln1_g[i], ln1_b[i], alpha)

        mk = mm(mem_bf, xa_wk[i].astype(BF16))
        mv = mm(mem_bf, xa_wv[i].astype(BF16))
        hd = d // MEM_HEADS
        outs["mk_p"].append(mk.reshape(bp, n_mem, MEM_HEADS, hd))
        outs["mv_p"].append(mv.reshape(bp, n_mem, MEM_HEADS, hd))
        q = mm(h_bf, xa_wq[i].astype(BF16), out_dtype=BF16)
        a_p = attention(q[:tp].reshape(bp, seq, d), mk.reshape(bp, n_mem, d), mv.reshape(bp, n_mem, d))
        a_s = attention(q[tp:].reshape(bs, 1, d), cache_mem_k[i].reshape(bs, n_mem, d),
                        cache_mem_v[i].reshape(bs, n_mem, d))
        att = jnp.concatenate([a_p.reshape(tp, d), a_s.reshape(bs, d)], axis=0)
        h_f32, h_bf = ln_res(h_f32, mm(att, xa_wo[i].astype(BF16)), ln2_g[i], ln2_b[i], alpha)

        gate_t = router(h_f32, jnp.transpose(router_w[i]), router_bias[i].reshape(ne, 1))
        gate = jnp.concatenate([jnp.transpose(gate_t), jnp.ones((tp + bs, 1), F32)], axis=1)
        gate = _pad_lanes(gate)
        wg = jnp.concatenate([exp_w_gate[i], sh_w_gate[i][None]], axis=0).astype(BF16)
        wu = jnp.concatenate([exp_w_up[i], sh_w_up[i][None]], axis=0).astype(BF16)
        wd = jnp.concatenate([exp_w_down[i], sh_w_down[i][None]], axis=0).astype(BF16)
        moe_out = moe_dense(h_bf, gate, wg, wu, wd)
        h_f32, h_bf = ln_res(h_f32, moe_out, ln3_g[i], ln3_b[i], alpha)

    st = lambda k: jnp.stack(outs[k])
    return (h_f32[:tp].reshape(bp, seq, d).astype(x_prompt.dtype),
            h_f32[tp:].reshape(bs, 1, d).astype(x_sample.dtype),
            st("dn_p"), st("dnc_p"), st("ssm_p"), st("ssmc_p"), st("mk_p"), st("mv_p"),
            st("dn_s"), st("dnc_s"), st("ssm_s"), st("ssmc_s"))
```

```python
import functools

import jax
import jax.numpy as jnp
from jax import lax
from jax.experimental import pallas as pl
from jax.experimental.pallas import tpu as pltpu

F32 = jnp.float32
BF16 = jnp.bfloat16

CONV_WIDTH = 4
CHUNK = 64
DN_HEAD = 128
SSM_HEAD_DIM = 64
SSM_GROUPS = 8
SSM_D_STATE = 128
MEM_HEADS = 4
N_EXPERT_GROUPS = 8
TOPK_GROUPS = 4
TOP_K = 8
ROUTED_SCALE = 2.5
LN_EPS = 1e-5
RMS_EPS = 1e-6
L2_EPS = 1e-6
LANES = 128
VMEM_LIMIT = 56 * 1024 * 1024


def _pick(n, cands):
    for c in cands:
        if n % c == 0:
            return c
    raise ValueError(f"no tile for {n} in {cands}")


def _params(*sem):
    return pltpu.CompilerParams(dimension_semantics=sem, vmem_limit_bytes=VMEM_LIMIT)


def _sigmoid(x):
    return 1.0 / (1.0 + jnp.exp(-x))


def _silu(x):
    return x * _sigmoid(x)


def _softplus(x):
    return jnp.maximum(x, 0.0) + jnp.log1p(jnp.exp(-jnp.abs(x)))


def _dot(a, b):
    return jnp.dot(a.astype(BF16), b.astype(BF16), preferred_element_type=F32)


def _dot_nt(a, b):
    return lax.dot_general(a.astype(BF16), b.astype(BF16), (((1,), (1,)), ((), ())),
                           preferred_element_type=F32)


def _dot_tn(a, b):
    return lax.dot_general(a.astype(BF16), b.astype(BF16), (((0,), (0,)), ((), ())),
                           preferred_element_type=F32)


def _split(x):
    hi = x.astype(BF16)
    lo = (x - hi.astype(F32)).astype(BF16)
    return hi, lo


def _dot_exact_lhs(a01, b):
    hi, lo = _split(b)
    a = a01.astype(BF16)
    return (jnp.dot(a, hi, preferred_element_type=F32) + jnp.dot(a, lo, preferred_element_type=F32))


def _tri(c, kind):
    r = lax.broadcasted_iota(jnp.int32, (c, c), 0)
    q = lax.broadcasted_iota(jnp.int32, (c, c), 1)
    if kind == "lower":
        return r >= q
    if kind == "strict":
        return r > q
    if kind == "upper":
        return r <= q
    raise ValueError(kind)


CAST_ROWS = 256


def _mm_kernel(x_ref, w_ref, *rest, lane_off):
    if lane_off:
        w2_ref, o_ref, wb_ref = rest
    else:
        o_ref, wb_ref = rest
    tn = wb_ref.shape[1]

    @pl.when(pl.program_id(1) == 0)
    def _():
        for r in range(0, w_ref.shape[0], CAST_ROWS):
            rows = slice(r, r + CAST_ROWS)
            if lane_off:
                wide = jnp.concatenate([w_ref[rows, :], w2_ref[rows, :]], axis=1)
                wb_ref[rows, :] = wide[:, lane_off:lane_off + tn].astype(BF16)
            else:
                wb_ref[rows, :] = w_ref[rows, :].astype(BF16)

    o_ref[...] = jnp.dot(x_ref[...], wb_ref[...], preferred_element_type=F32).astype(o_ref.dtype)


def mm(x, w_stack, layer, col0=0, n_cols=None, out_dtype=F32):
    m, k = x.shape
    n_total = w_stack.shape[2]
    n = n_total - col0 if n_cols is None else n_cols
    tm = _pick(m, (640, 512, 256, 128, 64, 32, 16, 8))
    tn = _pick(n, (1024, 512, 256, 128) if k <= 2048 else (512, 256, 128))
    lane_off = col0 % tn
    assert lane_off <= LANES and lane_off % 64 == 0
    base = col0 // tn
    in_specs = [pl.BlockSpec((tm, k), lambda j, i: (i, 0)),
                pl.BlockSpec((None, k, tn), lambda j, i: (layer, 0, base + j))]
    args = [x, w_stack]
    if lane_off:
        per = tn // LANES
        in_specs.append(pl.BlockSpec((None, k, LANES), lambda j, i: (layer, 0, (base + j + 1) * per)))
        args.append(w_stack)
    return pl.pallas_call(
        functools.partial(_mm_kernel, lane_off=lane_off),
        out_shape=jax.ShapeDtypeStruct((m, n), out_dtype),
        grid=(n // tn, m // tm),
        in_specs=in_specs,
        out_specs=pl.BlockSpec((tm, tn), lambda j, i: (i, j)),
        scratch_shapes=[pltpu.VMEM((k, tn), BF16)],
        compiler_params=_params("parallel", "arbitrary"),
        name="mm",
    )(*args)


def _ln_kernel(res_ref, *rest, alpha):
    *y_refs, g_ref, b_ref, of_ref, ob_ref = rest
    y = y_refs[0][...]
    for y_ref in y_refs[1:]:
        y = y + y_ref[...]
    x = alpha * res_ref[...] + y
    mu = jnp.mean(x, axis=-1, keepdims=True)
    xc = x - mu
    var = jnp.mean(xc * xc, axis=-1, keepdims=True)
    out = xc * lax.rsqrt(var + LN_EPS) * g_ref[...] + b_ref[...]
    of_ref[...] = out
    ob_ref[...] = out.astype(BF16)


def ln_res(res, ys, g, b, alpha):
    t, d = res.shape
    tm = _pick(t, (320, 256, 128, 64, 32, 16, 8))
    row = pl.BlockSpec((tm, d), lambda i: (i, 0))
    vec = pl.BlockSpec((1, d), lambda i: (0, 0))
    return pl.pallas_call(
        functools.partial(_ln_kernel, alpha=alpha),
        out_shape=(jax.ShapeDtypeStruct((t, d), F32), jax.ShapeDtypeStruct((t, d), BF16)),
        grid=(t // tm,),
        in_specs=[row] * (1 + len(ys)) + [vec, vec],
        out_specs=(row, row),
        compiler_params=_params("parallel"),
        name="ln_res",
    )(res, *ys, g.reshape(1, d), b.reshape(1, d))


def _merge_kernel(ga_ref, gb_ref, ya_ref, yb_ref, o_ref):
    o = _sigmoid(ga_ref[...]) * ya_ref[...] + _sigmoid(gb_ref[...]) * yb_ref[...]
    o_ref[...] = o.astype(o_ref.dtype)


def gate_merge(proj, gate_off, ya, yb):
    t, d = ya.shape
    tm = _pick(t, (320, 256, 128, 64, 32, 16, 8))
    tc = 512
    nb = d // tc
    off = gate_off // tc
    return pl.pallas_call(
        _merge_kernel,
        out_shape=jax.ShapeDtypeStruct((t, d), BF16),
        grid=(t // tm, nb),
        in_specs=[pl.BlockSpec((tm, tc), lambda i, j: (i, off + j)),
                  pl.BlockSpec((tm, tc), lambda i, j: (i, off + nb + j)),
                  pl.BlockSpec((tm, tc), lambda i, j: (i, j)),
                  pl.BlockSpec((tm, tc), lambda i, j: (i, j))],
        out_specs=pl.BlockSpec((tm, tc), lambda i, j: (i, j)),
        compiler_params=_params("parallel", "parallel"),
        name="gate_merge",
    )(proj, proj, ya, yb)


def _gates_kernel(pa_ref, pb_ref, bias_ref, alog_ref, o1_ref, o2_ref, *, n_beta, n_g):
    lane = lax.broadcasted_iota(jnp.int32, pa_ref.shape, 1)
    p = jnp.where(lane < n_beta + n_g, pa_ref[...], pb_ref[...])
    sp = _softplus(p + bias_ref[...])
    neg_a = -jnp.exp(alog_ref[...])
    o1_ref[...] = jnp.where(lane < n_beta, _sigmoid(p), jnp.where(lane < n_beta + n_g, neg_a * sp, sp))
    o2_ref[...] = neg_a * sp


def gates(p_ba, p_dt, bias_row, alog_row, n_beta, n_g):
    t, w = p_ba.shape
    tm = _pick(t, (640, 512, 256, 128, 64, 32, 16, 8))
    row = pl.BlockSpec((tm, w), lambda i: (i, 0))
    vec = pl.BlockSpec((1, w), lambda i: (0, 0))
    return pl.pallas_call(
        functools.partial(_gates_kernel, n_beta=n_beta, n_g=n_g),
        out_shape=(jax.ShapeDtypeStruct((t, w), F32), jax.ShapeDtypeStruct((t, w), F32)),
        grid=(t // tm,),
        in_specs=[row, row, vec, vec],
        out_specs=(row, row),
        compiler_params=_params("parallel"),
        name="gates",
    )(p_ba, p_dt, bias_row, alog_row)


def _head_l2norm(y, scale_row):
    parts = []
    for g in range(y.shape[1] // DN_HEAD):
        seg = y[:, g * DN_HEAD:(g + 1) * DN_HEAD]
        ss = jnp.sum(seg * seg, axis=-1, keepdims=True)
        parts.append(seg * lax.rsqrt(ss + L2_EPS))
    return jnp.concatenate(parts, axis=1) * scale_row


def _conv_prompt_kernel(x_ref, w_ref, b_ref, s_ref, o_ref, *, norm):
    x = x_ref[...]
    row = lax.broadcasted_iota(jnp.int32, x.shape, 0)
    acc = x * w_ref[CONV_WIDTH - 1:CONV_WIDTH, :]
    for j in range(1, CONV_WIDTH):
        xs = jnp.where(row >= j, pltpu.roll(x, j, axis=0), 0.0)
        acc = acc + xs * w_ref[CONV_WIDTH - 1 - j:CONV_WIDTH - j, :]
    y = _silu(acc + b_ref[...])
    if norm:
        y = _head_l2norm(y, s_ref[...])
    o_ref[...] = y


def conv_prompt(proj, col_off, width, seq, batch, w, bias, scale, norm):
    tc = 256
    off = col_off // tc
    vec = lambda r: pl.BlockSpec((r, tc), lambda b, j: (0, j))
    return pl.pallas_call(
        functools.partial(_conv_prompt_kernel, norm=norm),
        out_shape=jax.ShapeDtypeStruct((batch * seq, width), F32),
        grid=(batch, width // tc),
        in_specs=[pl.BlockSpec((seq, tc), lambda b, j: (b, off + j)), vec(CONV_WIDTH), vec(1), vec(1)],
        out_specs=pl.BlockSpec((seq, tc), lambda b, j: (b, j)),
        compiler_params=_params("parallel", "parallel"),
        name="conv_prompt",
    )(proj, w, bias, scale)


def _conv_step_kernel(x_ref, buf_ref, w_ref, b_ref, s_ref, o_ref, *, norm):
    acc = x_ref[...] * w_ref[CONV_WIDTH - 1:CONV_WIDTH, :]
    for j in range(CONV_WIDTH - 1):
        acc = acc + buf_ref[j] * w_ref[j:j + 1, :]
    y = _silu(acc + b_ref[...])
    if norm:
        y = _head_l2norm(y, s_ref[...])
    o_ref[...] = y


def conv_step(proj, row_off, nrows, col_off, width, buf_t, w, bias, scale, norm):
    tc = 512
    off = col_off // tc
    rb = row_off // nrows
    vec = lambda r: pl.BlockSpec((r, tc), lambda j: (0, j))
    return pl.pallas_call(
        functools.partial(_conv_step_kernel, norm=norm),
        out_shape=jax.ShapeDtypeStruct((nrows, width), F32),
        grid=(width // tc,),
        in_specs=[pl.BlockSpec((nrows, tc), lambda j: (rb, off + j)),
                  pl.BlockSpec((CONV_WIDTH - 1, nrows, tc), lambda j: (0, 0, j)),
                  vec(CONV_WIDTH), vec(1), vec(1)],
        out_specs=pl.BlockSpec((nrows, tc), lambda j: (0, j)),
        compiler_params=_params("parallel"),
        name="conv_step",
    )(proj, buf_t, w, bias, scale)


def _gated_rms(o, nw_row, z):
    ms = jnp.mean(o * o, axis=-1, keepdims=True)
    return o * lax.rsqrt(ms + RMS_EPS) * nw_row * _silu(z)


def _delta_prompt_kernel(q_ref, k_ref, v_ref, z_ref, g_ref, nw_ref, o_ref, s_ref, *, n_pairs, n_vheads):
    c = q_ref.shape[0]
    c2 = 2 * c
    H = DN_HEAD
    n = pl.program_id(1)

    @pl.when(n == 0)
    def _():
        s_ref[...] = jnp.zeros_like(s_ref)

    row = lax.broadcasted_iota(jnp.int32, (c2, c2), 0)
    col = lax.broadcasted_iota(jnp.int32, (c2, c2), 1)
    same = (row >= c) == (col >= c)
    low = same & (row >= col)
    strict = same & (row > col)
    eye = jnp.where(row == col, 1.0, 0.0).astype(F32)
    top = lax.broadcasted_iota(jnp.int32, (c2, H), 0) < c
    top1 = lax.broadcasted_iota(jnp.int32, (c2, 1), 0) < c
    top_s = lax.broadcasted_iota(jnp.int32, (2 * H, 1), 0) < H

    gblk = g_ref[...]
    cum = _dot_exact_lhs(jnp.where(_tri(c, "lower"), 1.0, 0.0), gblk)
    stack = lambda a, i, j: jnp.concatenate([a[:, i:i + 1], a[:, j:j + 1]], axis=0)
    nw = nw_ref[...]

    P = range(n_pairs)
    beta = [stack(gblk, 2 * p, 2 * p + 1) for p in P]
    gc = [stack(cum, n_vheads + 2 * p, n_vheads + 2 * p + 1) for p in P]
    k2 = [jnp.concatenate([k_ref[:, p * H:(p + 1) * H]] * 2, axis=0) for p in P]
    q2 = [jnp.concatenate([q_ref[:, p * H:(p + 1) * H]] * 2, axis=0) for p in P]
    v2 = [jnp.concatenate([v_ref[:, 2 * p * H:(2 * p + 1) * H], v_ref[:, (2 * p + 1) * H:(2 * p + 2) * H]], axis=0)
          for p in P]
    kb = [k2[p] * beta[p] for p in P]
    egc = [jnp.exp(gc[p]) for p in P]

    decay = []
    for p in P:
        x = jnp.broadcast_to(gc[p], (c2, c2))
        decay.append(jnp.where(low, jnp.exp(jnp.where(low, x - x.T, 0.0)), 0.0))

    m1 = [_dot_nt(jnp.concatenate([kb[p], q2[p]], axis=0), k2[p]) for p in P]
    pw = [jnp.where(strict, -m1[p][:c2] * decay[p], 0.0) for p in P]
    attn = [m1[p][c2:] * decay[p] for p in P]
    inv = [eye + pw[p] for p in P]
    m = 2
    while m < c:
        pw = [_dot(pw[p], pw[p]) for p in P]
        inv = [inv[p] + _dot(inv[p], pw[p]) for p in P]
        m *= 2
    sol = [_dot(inv[p], jnp.concatenate([v2[p] * beta[p], kb[p] * egc[p]], axis=1)) for p in P]

    def blockdiag(a):
        return jnp.concatenate([jnp.where(top, a, 0.0), jnp.where(top, 0.0, a)], axis=1)

    s2 = [jnp.concatenate([s_ref[2 * p], s_ref[2 * p + 1]], axis=0) for p in P]
    m2 = [_dot(jnp.concatenate([blockdiag(sol[p][:, H:]), blockdiag(q2[p] * egc[p])], axis=0), s2[p]) for p in P]
    v_new = [sol[p][:, :H] - m2[p][:c2] for p in P]
    o = [m2[p][c2:] + _dot(attn[p], v_new[p]) for p in P]
    for p in P:
        g_last = jnp.where(top1, gc[p][c - 1:c, :], gc[p][c2 - 1:c2, :])
        kd = k2[p] * jnp.exp(g_last - gc[p])
        es = jnp.where(top_s, jnp.exp(gc[p][c - 1:c, :]), jnp.exp(gc[p][c2 - 1:c2, :]))
        s_new = s2[p] * es + _dot_tn(blockdiag(kd), v_new[p])
        s_ref[2 * p] = s_new[:H]
        s_ref[2 * p + 1] = s_new[H:]
        z2 = jnp.concatenate([z_ref[:, 2 * p * H:(2 * p + 1) * H], z_ref[:, (2 * p + 1) * H:(2 * p + 2) * H]], axis=0)
        out = _gated_rms(o[p], nw, z2).astype(o_ref.dtype)
        o_ref[:, 2 * p * H:(2 * p + 1) * H] = out[:c]
        o_ref[:, (2 * p + 1) * H:(2 * p + 2) * H] = out[c:]


def delta_prompt(qk, v, proj, z_off, g1, norm_w, batch, seq, n_vheads):
    c = CHUNK
    nc = seq // c
    hk = n_vheads // 2
    qw = hk * DN_HEAD
    vw = n_vheads * DN_HEAD
    zb = z_off // vw
    rowblk = lambda b, n: b * nc + n
    return pl.pallas_call(
        functools.partial(_delta_prompt_kernel, n_pairs=hk, n_vheads=n_vheads),
        out_shape=(jax.ShapeDtypeStruct((batch * seq, vw), BF16),
                   jax.ShapeDtypeStruct((batch, n_vheads, DN_HEAD, DN_HEAD), F32)),
        grid=(batch, nc),
        in_specs=[pl.BlockSpec((c, qw), lambda b, n: (rowblk(b, n), 0)),
                  pl.BlockSpec((c, qw), lambda b, n: (rowblk(b, n), 1)),
                  pl.BlockSpec((c, vw), lambda b, n: (rowblk(b, n), 0)),
                  pl.BlockSpec((c, vw), lambda b, n: (rowblk(b, n), zb)),
                  pl.BlockSpec((c, LANES), lambda b, n: (rowblk(b, n), 0)),
                  pl.BlockSpec((1, DN_HEAD), lambda b, n: (0, 0))],
        out_specs=(pl.BlockSpec((c, vw), lambda b, n: (rowblk(b, n), 0)),
                   pl.BlockSpec((None, n_vheads, DN_HEAD, DN_HEAD), lambda b, n: (b, 0, 0, 0))),
        compiler_params=_params("parallel", "arbitrary"),
        name="delta_prompt",
    )(qk, qk, v, proj, g1, norm_w)


def _stacked_state_call(kern, layer, state, prev_out, in_specs, args, o_shape, o_spec, grid, name):
    blk = (None, 1) + state.shape[2:]
    st = pl.BlockSpec(blk, lambda i: (layer, i) + (0,) * (len(blk) - 2))
    in_specs = list(in_specs) + [st]
    args = list(args) + [state]
    aliases = {}
    if prev_out is not None:
        in_specs.append(pl.BlockSpec(memory_space=pl.ANY))
        args.append(prev_out)
        aliases = {len(args) - 1: 1}
    return pl.pallas_call(
        kern,
        out_shape=(o_shape, jax.ShapeDtypeStruct(state.shape, F32)),
        grid=grid,
        in_specs=in_specs,
        out_specs=(o_spec, st),
        input_output_aliases=aliases,
        compiler_params=_params("parallel"),
        name=name,
    )(*args)


def _delta_step_kernel(qt_ref, kt_ref, v_ref, z_ref, gt_ref, nw_ref, s_ref, *rest, n_vheads):
    o_ref, so_ref = rest[-2:]
    qt = qt_ref[0]
    kt = kt_ref[0]
    gt = gt_ref[0]
    for h in range(n_vheads):
        kh = h // 2
        qc = qt[:, kh:kh + 1]
        kc = kt[:, kh:kh + 1]
        v = v_ref[0, :, h * DN_HEAD:(h + 1) * DN_HEAD]
        beta = gt[:, h:h + 1]
        eg = jnp.exp(gt[:, n_vheads + h:n_vheads + h + 1])
        s = s_ref[0, h]
        ks = jnp.sum(kc * s, axis=0, keepdims=True)
        qs = jnp.sum(qc * s, axis=0, keepdims=True)
        qk = jnp.sum(qc * kc, axis=0, keepdims=True)
        v_new = beta * v - (beta * eg) * ks
        o = eg * qs + qk * v_new
        so_ref[0, h] = s * eg + kc * v_new
        z = z_ref[0, :, h * DN_HEAD:(h + 1) * DN_HEAD]
        o_ref[0, :, h * DN_HEAD:(h + 1) * DN_HEAD] = _gated_rms(o, nw_ref[...], z).astype(o_ref.dtype)


def delta_step(qt, kt, v3, z3, g3, norm_w, state, layer, prev_out):
    nb, n_vheads = state.shape[1], state.shape[2]
    hk = qt.shape[2]
    vw = n_vheads * DN_HEAD
    blk3 = lambda a, b_: pl.BlockSpec((1, a, b_), lambda i: (i, 0, 0))
    return _stacked_state_call(
        functools.partial(_delta_step_kernel, n_vheads=n_vheads), layer, state, prev_out,
        [blk3(DN_HEAD, hk), blk3(DN_HEAD, hk), blk3(1, vw), blk3(1, vw), blk3(1, LANES),
         pl.BlockSpec((1, DN_HEAD), lambda i: (0, 0))],
        [qt, kt, v3, z3, g3, norm_w],
        jax.ShapeDtypeStruct((nb, 1, vw), BF16), blk3(1, vw), (nb,), "delta_step")


def _ssd_finish(y, xs, z, d_row, nw_row):
    y = (y + d_row * xs) * _silu(z)
    ms = jnp.mean(y * y, axis=-1, keepdims=True)
    return y * lax.rsqrt(ms + RMS_EPS) * nw_row


def _ssd_prompt_kernel(x_ref, b_ref, c_ref, z_ref, g1_ref, g2_ref, d_ref, nw_ref, o_ref, s_ref, *, n_heads, lane0):
    c = x_ref.shape[0]
    c2 = 2 * c
    P = SSM_HEAD_DIM
    N = SSM_D_STATE
    R = n_heads // SSM_GROUPS
    gw = R * P
    n = pl.program_id(1)

    @pl.when(n == 0)
    def _():
        s_ref[...] = jnp.zeros_like(s_ref)

    dtm = g1_ref[...]
    cum = _dot_exact_lhs(jnp.where(_tri(c, "lower"), 1.0, 0.0), g2_ref[...])
    left = lax.broadcasted_iota(jnp.int32, (c, 2 * P), 1) < P
    rowi = lax.broadcasted_iota(jnp.int32, (c, c2), 0)
    colj = lax.broadcasted_iota(jnp.int32, (c, c2), 1)
    first = colj < c
    causal = rowi >= jnp.where(first, colj, colj - c)
    top_s = lax.broadcasted_iota(jnp.int32, (2 * P, 1), 0) < P

    G = range(SSM_GROUPS)
    bm = [b_ref[:, g * N:(g + 1) * N] for g in G]
    cm = [c_ref[:, g * N:(g + 1) * N] for g in G]
    cb2 = [_dot_nt(cm[g], jnp.concatenate([bm[g], bm[g]], axis=0)) for g in G]
    y2 = [_dot_nt(cm[g], s_ref[g * R:(g + 1) * R].reshape(gw, N)) for g in G]

    pairs = [(g, pr) for g in G for pr in range(R // 2)]
    col = lambda a, h: a[:, lane0 + h:lane0 + h + 1]
    cc, xdt, y, upd = {}, {}, {}, {}
    for g, pr in pairs:
        h0 = g * R + 2 * pr
        cc[g, pr] = (col(cum, h0), col(cum, h0 + 1))
        dt2 = jnp.where(left, col(dtm, h0), col(dtm, h0 + 1))
        xdt[g, pr] = x_ref[:, h0 * P:(h0 + 2) * P] * dt2
    for g, pr in pairs:
        c0, c1 = cc[g, pr]
        x = jnp.broadcast_to(jnp.concatenate([c0, c1], axis=0), (c2, c2))
        cr2 = x.T[:c, :]
        seg = jnp.where(causal, jnp.exp(jnp.where(causal, jnp.where(first, c0, c1) - cr2, 0.0)), 0.0)
        xd = xdt[g, pr]
        bd = jnp.concatenate([jnp.where(left, xd, 0.0), jnp.where(left, 0.0, xd)], axis=0)
        ecum = jnp.where(left, jnp.exp(c0), jnp.exp(c1))
        y[g, pr] = _dot(cb2[g] * seg, bd) + y2[g][:, pr * 2 * P:(pr + 1) * 2 * P] * ecum
    for g, pr in pairs:
        c0, c1 = cc[g, pr]
        l0, l1 = c0[c - 1:c, :], c1[c - 1:c, :]
        e2 = jnp.where(left, jnp.exp(l0 - c0), jnp.exp(l1 - c1))
        upd[g, pr] = (_dot_tn(xdt[g, pr] * e2, bm[g]), jnp.where(top_s, jnp.exp(l0), jnp.exp(l1)))
    for g, pr in pairs:
        h0 = g * R + 2 * pr
        u, el = upd[g, pr]
        s_new = s_ref[h0:h0 + 2].reshape(2 * P, N) * el + u
        s_ref[h0] = s_new[:P]
        s_ref[h0 + 1] = s_new[P:]
    for g in G:
        sl = slice(g * gw, (g + 1) * gw)
        yg = jnp.concatenate([y[g, pr] for pr in range(R // 2)], axis=1)
        o_ref[:, sl] = _ssd_finish(yg, x_ref[:, sl], z_ref[:, sl], d_ref[:, sl], nw_ref[:, sl]).astype(o_ref.dtype)


def ssd_prompt(xbc, proj, z_off, g1, g2, lane0, d_exp, norm_w, batch, seq, n_heads):
    c = CHUNK
    nc = seq // c
    inner = n_heads * SSM_HEAD_DIM
    bc_w = SSM_GROUPS * SSM_D_STATE
    b_off = inner // bc_w
    zb = z_off // inner
    rowblk = lambda b, n: b * nc + n
    vec = pl.BlockSpec((1, inner), lambda b, n: (0, 0))
    gate = pl.BlockSpec((c, LANES), lambda b, n: (rowblk(b, n), 0))
    return pl.pallas_call(
        functools.partial(_ssd_prompt_kernel, n_heads=n_heads, lane0=lane0),
        out_shape=(jax.ShapeDtypeStruct((batch * seq, inner), BF16),
                   jax.ShapeDtypeStruct((batch, n_heads, SSM_HEAD_DIM, SSM_D_STATE), F32)),
        grid=(batch, nc),
        in_specs=[pl.BlockSpec((c, inner), lambda b, n: (rowblk(b, n), 0)),
                  pl.BlockSpec((c, bc_w), lambda b, n: (rowblk(b, n), b_off)),
                  pl.BlockSpec((c, bc_w), lambda b, n: (rowblk(b, n), b_off + 1)),
                  pl.BlockSpec((c, inner), lambda b, n: (rowblk(b, n), zb)),
                  gate, gate, vec, vec],
        out_specs=(pl.BlockSpec((c, inner), lambda b, n: (rowblk(b, n), 0)),
                   pl.BlockSpec((None, n_heads, SSM_HEAD_DIM, SSM_D_STATE), lambda b, n: (b, 0, 0, 0))),
        compiler_params=_params("parallel", "arbitrary"),
        name="ssd_prompt",
    )(xbc, xbc, xbc, proj, g1, g2, d_exp, norm_w)


def _ssd_step_kernel(xt_ref, xr_ref, b_ref, c_ref, z_ref, g1_ref, g2_ref, d_ref, nw_ref, s_ref,
                     *rest, n_heads):
    o_ref, so_ref = rest[-2:]
    R = n_heads // SSM_GROUPS
    P = SSM_HEAD_DIM
    gw = R * P
    xt = xt_ref[0]
    g1 = g1_ref[0]
    g2 = g2_ref[0]
    off = LANES - n_heads
    for g in range(SSM_GROUPS):
        brow = b_ref[0, g:g + 1, :]
        for r in range(R):
            h = g * R + r
            dt = g1[:, off + h:off + h + 1]
            eda = jnp.exp(g2[:, off + h:off + h + 1])
            so_ref[0, h] = s_ref[0, h] * eda + (xt[:, h:h + 1] * dt) * brow
        s_new = so_ref[0, g * R:(g + 1) * R].reshape(gw, SSM_D_STATE)
        c8 = jnp.broadcast_to(c_ref[0, g:g + 1, :], (8, SSM_D_STATE))
        y = _dot_nt(c8, s_new)[0:1, :]
        sl = slice(g * gw, (g + 1) * gw)
        o_ref[0, :, sl] = _ssd_finish(y, xr_ref[0, :, sl], z_ref[0, :, sl], d_ref[:, sl],
                                      nw_ref[:, sl]).astype(o_ref.dtype)


def ssd_step(xt, x3, b3, c3, z3, g1, g2, d_exp, norm_w, state, layer, prev_out):
    nb, n_heads = state.shape[1], state.shape[2]
    inner = n_heads * SSM_HEAD_DIM
    blk3 = lambda a, b_: pl.BlockSpec((1, a, b_), lambda i: (i, 0, 0))
    vec = pl.BlockSpec((1, inner), lambda i: (0, 0))
    return _stacked_state_call(
        functools.partial(_ssd_step_kernel, n_heads=n_heads), layer, state, prev_out,
        [blk3(SSM_HEAD_DIM, n_heads), blk3(1, inner), blk3(SSM_GROUPS, SSM_D_STATE),
         blk3(SSM_GROUPS, SSM_D_STATE), blk3(1, inner), blk3(1, LANES), blk3(1, LANES), vec, vec],
        [xt, x3, b3, c3, z3, g1, g2, d_exp, norm_w],
        jax.ShapeDtypeStruct((nb, 1, inner), BF16), blk3(1, inner), (nb,), "ssd_step")


def _attn_kernel(q_ref, k_ref, v_ref, o_ref):
    d = q_ref.shape[2]
    hd = d // MEM_HEADS
    scale = hd ** -0.5
    for h in range(MEM_HEADS):
        sl = slice(h * hd, (h + 1) * hd)
        s = _dot_nt(q_ref[0, :, sl], k_ref[0, :, sl]) * scale
        s = s - jnp.max(s, axis=-1, keepdims=True)
        p = jnp.exp(s)
        p = p / jnp.sum(p, axis=-1, keepdims=True)
        o_ref[0, :, sl] = _dot(p, v_ref[0, :, sl]).astype(o_ref.dtype)


def attention(q, k, v, layer):
    b, l, d = q.shape
    m = k.shape[2]
    tl = _pick(l, (512, 256, 128, 64, 32, 16, 8, 1))
    kv = pl.BlockSpec((None, 1, m, d), lambda i, j: (layer, i, 0, 0))
    return pl.pallas_call(
        _attn_kernel,
        out_shape=jax.ShapeDtypeStruct((b, l, d), BF16),
        grid=(b, l // tl),
        in_specs=[pl.BlockSpec((1, tl, d), lambda i, j: (i, j, 0)), kv, kv],
        out_specs=pl.BlockSpec((1, tl, d), lambda i, j: (i, j, 0)),
        compiler_params=_params("parallel", "arbitrary"),
        name="attention",
    )(q, k, v)


def _router_kernel(t_ref, w_ref, b_ref, o_ref):
    t = t_ref[...]
    w = w_ref[...]
    th, tl = _split(t)
    wh, wl = _split(w)
    nt = lambda a, b_: lax.dot_general(a, b_, (((1,), (1,)), ((), ())), preferred_element_type=F32)
    logits = nt(wh, th) + (nt(wh, tl) + nt(wl, th))
    scores = _sigmoid(logits)
    sel = scores + b_ref[...]
    ne, tm = sel.shape
    epg = ne // N_EXPERT_GROUPS
    neg = -jnp.inf

    s3 = sel.reshape(N_EXPERT_GROUPS, epg, tm)
    i3 = lax.broadcasted_iota(jnp.int32, s3.shape, 1)
    m1 = jnp.max(s3, axis=1, keepdims=True)
    a1 = jnp.min(jnp.where(s3 == m1, i3, epg), axis=1, keepdims=True)
    m2 = jnp.max(jnp.where(i3 == a1, neg, s3), axis=1, keepdims=True)
    grp = (m1 + m2).reshape(N_EXPERT_GROUPS, tm)

    gi = lax.broadcasted_iota(jnp.int32, grp.shape, 0)
    gsel = jnp.zeros(grp.shape, jnp.bool_)
    for _ in range(TOPK_GROUPS):
        mx = jnp.max(grp, axis=0, keepdims=True)
        am = jnp.min(jnp.where(grp == mx, gi, N_EXPERT_GROUPS), axis=0, keepdims=True)
        hit = gi == am
        gsel = gsel | hit
        grp = jnp.where(hit, neg, grp)

    gmask = jnp.broadcast_to(gsel.reshape(N_EXPERT_GROUPS, 1, tm), s3.shape).reshape(ne, tm)
    cur = jnp.where(gmask, sel, neg)
    ei = lax.broadcasted_iota(jnp.int32, cur.shape, 0)
    chosen = jnp.zeros(cur.shape, jnp.bool_)
    for _ in range(TOP_K):
        mx = jnp.max(cur, axis=0, keepdims=True)
        am = jnp.min(jnp.where(cur == mx, ei, ne), axis=0, keepdims=True)
        hit = ei == am
        chosen = chosen | hit
        cur = jnp.where(hit, neg, cur)

    wts = jnp.where(chosen, scores, 0.0)
    o_ref[...] = wts / jnp.sum(wts, axis=0, keepdims=True) * ROUTED_SCALE


def router(t, w_t, bias_col):
    tt, d = t.shape
    ne = w_t.shape[0]
    tm = _pick(tt, (640, 512, 256, 128))
    return pl.pallas_call(
        _router_kernel,
        out_shape=jax.ShapeDtypeStruct((ne, tt), F32),
        grid=(tt // tm,),
        in_specs=[pl.BlockSpec((tm, d), lambda i: (i, 0)),
                  pl.BlockSpec((ne, d), lambda i: (0, 0)),
                  pl.BlockSpec((ne, 1), lambda i: (0, 0))],
        out_specs=pl.BlockSpec((ne, tm), lambda i: (0, i)),
        compiler_params=_params("parallel"),
        name="router",
    )(t, w_t, bias_col)


def _moe_kernel(t_ref, g_ref, wg_ref, wu_ref, wd_ref, o_ref):
    e = pl.program_id(1)

    @pl.when(e == 0)
    def _():
        o_ref[...] = jnp.zeros_like(o_ref)

    t = t_ref[...]
    gate = g_ref[...]
    lane = lax.broadcasted_iota(jnp.int32, gate.shape, 1)
    gcol = jnp.sum(jnp.where(lane == e, gate, 0.0), axis=1, keepdims=True)
    hid = _silu(jnp.dot(t, wg_ref[0], preferred_element_type=F32)) * jnp.dot(
        t, wu_ref[0], preferred_element_type=F32)
    o_ref[...] += _dot(hid * gcol, wd_ref[0])


def moe_dense(t, gate, wg, wu, wd, layer):
    tt, d = t.shape
    _, ne, _, ff = wg.shape
    tm = _pick(tt, (640, 512, 256, 128))
    return pl.pallas_call(
        _moe_kernel,
        out_shape=jax.ShapeDtypeStruct((tt, d), F32),
        grid=(tt // tm, ne),
        in_specs=[pl.BlockSpec((tm, d), lambda i, e: (i, 0)),
                  pl.BlockSpec((tm, LANES), lambda i, e: (i, 0)),
                  pl.BlockSpec((None, 1, d, ff), lambda i, e: (layer, e, 0, 0)),
                  pl.BlockSpec((None, 1, d, ff), lambda i, e: (layer, e, 0, 0)),
                  pl.BlockSpec((None, 1, ff, d), lambda i, e: (layer, e, 0, 0))],
        out_specs=pl.BlockSpec((tm, d), lambda i, e: (i, 0)),
        compiler_params=_params("parallel", "arbitrary"),
        name="moe_dense",
    )(t, gate, wg, wu, wd)


def _pad_lanes(a, width=LANES):
    return jnp.pad(a, [(0, 0)] * (a.ndim - 1) + [(0, width - a.shape[-1])])


def kernel(x_prompt, x_sample, mem_prompt, state_dn, state_dn_conv, state_ssm, state_ssm_conv, cache_mem_k, cache_mem_v, w_in, dn_conv_w, dn_a_log, dn_dt_bias, dn_norm_w, dn_w_out, ssm_conv_w, ssm_conv_b, ssm_a_log, ssm_dt_bias, ssm_d, ssm_norm_w, ssm_w_out, w_o, ln1_g, ln1_b, xa_wq, xa_wk, xa_wv, xa_wo, ln2_g, ln2_b, router_w, router_bias, exp_w_gate, exp_w_up, exp_w_down, sh_w_gate, sh_w_up, sh_w_down, ln3_g, ln3_b):
    bp, seq, d = x_prompt.shape
    bs = x_sample.shape[0]
    depth = w_in.shape[0]
    tp = bp * seq
    n_mem = mem_prompt.shape[1]
    hv = state_dn.shape[2]
    hk = hv // 2
    qk_w = hk * DN_HEAD
    v_w = hv * DN_HEAD
    dn_conv = 2 * qk_w + v_w
    hs = state_ssm.shape[2]
    inner = hs * SSM_HEAD_DIM
    bc_w = SSM_GROUPS * SSM_D_STATE
    ssm_conv = inner + 2 * bc_w
    ne = router_w.shape[2]
    alpha = (2 * depth) ** 0.25

    o_z, o_b = dn_conv, dn_conv + v_w
    o_xbc = o_b + 2 * hv
    o_sz = o_xbc + ssm_conv
    o_dt = o_sz + inner
    o_gate = o_dt + hs
    assert o_b % LANES == 0 and 2 * hv + hs == LANES and (o_dt + hs) % LANES == 0

    h_f32 = jnp.concatenate([x_prompt.reshape(tp, d), x_sample.reshape(bs, d)], axis=0)
    h_bf = h_f32.astype(BF16)
    mem_bf = mem_prompt.reshape(bp * n_mem, d).astype(BF16)
    mem_k_cache = cache_mem_k.reshape(depth, bs, n_mem, d)
    mem_v_cache = cache_mem_v.reshape(depth, bs, n_mem, d)
    hd = d // MEM_HEADS

    e_wg, e_wu, e_wd = exp_w_gate.astype(BF16), exp_w_up.astype(BF16), exp_w_down.astype(BF16)
    s_wg, s_wu, s_wd = (w[:, None].astype(BF16) for w in (sh_w_gate, sh_w_up, sh_w_down))
    ones_gate = jnp.ones((tp + bs, LANES), F32)

    outs = {k: [] for k in ("dn_p", "dnc_p", "ssm_p", "ssmc_p", "mk_p", "mv_p", "dnc_s", "ssmc_s")}
    dn_s_all = None
    ssm_s_all = None

    for i in range(depth):
        p_dn = mm(h_bf, w_in, i, 0, o_b)
        p_ba = mm(h_bf, w_in, i, o_b, LANES)
        p_xbc = mm(h_bf, w_in, i, o_xbc, ssm_conv)
        p_sz = mm(h_bf, w_in, i, o_sz, inner)
        p_dt = mm(h_bf, w_in, i, o_dt + hs - LANES, LANES)
        p_gate = mm(h_bf, w_in, i, o_gate, 2 * d)

        zeros_hv = jnp.zeros((hv,), F32)
        bias_row = jnp.concatenate([zeros_hv, dn_dt_bias[i], ssm_dt_bias[i]]).reshape(1, LANES)
        alog_row = jnp.concatenate([zeros_hv, dn_a_log[i], ssm_a_log[i]]).reshape(1, LANES)
        g1, g2 = gates(p_ba, p_dt, bias_row, alog_row, hv, hv)

        dn_scale = jnp.concatenate([jnp.full((qk_w,), DN_HEAD ** -0.5, F32), jnp.ones((qk_w,), F32)]).reshape(1, -1)
        dn_zero_b = jnp.zeros((1, dn_conv), F32)
        cw = dn_conv_w[i]
        qk_p = conv_prompt(p_dn, 0, 2 * qk_w, seq, bp, cw[:, :2 * qk_w], dn_zero_b[:, :2 * qk_w], dn_scale, True)
        v_p = conv_prompt(p_dn, 2 * qk_w, v_w, seq, bp, cw[:, 2 * qk_w:], dn_zero_b[:, 2 * qk_w:],
                          dn_zero_b[:, 2 * qk_w:], False)
        scw = ssm_conv_w[i]
        scb = ssm_conv_b[i].reshape(1, ssm_conv)
        xbc_p = conv_prompt(p_xbc, 0, ssm_conv, seq, bp, scw, scb, scb, False)

        dn_buf_t = jnp.transpose(state_dn_conv[i], (1, 0, 2))
        qk_s = conv_step(p_dn, tp, bs, 0, 2 * qk_w, dn_buf_t[:, :, :2 * qk_w], cw[:, :2 * qk_w],
                         dn_zero_b[:, :2 * qk_w], dn_scale, True)
        v_s = conv_step(p_dn, tp, bs, 2 * qk_w, v_w, dn_buf_t[:, :, 2 * qk_w:], cw[:, 2 * qk_w:],
                        dn_zero_b[:, 2 * qk_w:], dn_zero_b[:, 2 * qk_w:], False)
        ssm_buf_t = jnp.transpose(state_ssm_conv[i], (1, 0, 2))
        xbc_s = conv_step(p_xbc, tp, bs, 0, ssm_conv, ssm_buf_t, scw, scb, scb, False)

        tail = seq - (CONV_WIDTH - 1)
        outs["dnc_p"].append(p_dn[:tp, :dn_conv].reshape(bp, seq, dn_conv)[:, tail:])
        outs["ssmc_p"].append(p_xbc[:tp].reshape(bp, seq, ssm_conv)[:, tail:])
        outs["dnc_s"].append(jnp.concatenate([state_dn_conv[i][:, 1:], p_dn[tp:, None, :dn_conv]], axis=1))
        outs["ssmc_s"].append(jnp.concatenate([state_ssm_conv[i][:, 1:], p_xbc[tp:, None]], axis=1))

        nw_dn = dn_norm_w[i].reshape(1, DN_HEAD)
        o_dn_p, s_dn_p = delta_prompt(qk_p, v_p, p_dn, o_z, g1, nw_dn, bp, seq, hv)
        qt = jnp.transpose(qk_s[:, :qk_w].reshape(bs, hk, DN_HEAD), (0, 2, 1))
        kt = jnp.transpose(qk_s[:, qk_w:].reshape(bs, hk, DN_HEAD), (0, 2, 1))
        z_dn_s = p_dn[tp:, o_z:].reshape(bs, 1, v_w)
        o_dn_s, dn_s_all = delta_step(qt, kt, v_s.reshape(bs, 1, v_w), z_dn_s, g1[tp:].reshape(bs, 1, LANES),
                                      nw_dn, state_dn, i, dn_s_all)
        o_dn = jnp.concatenate([o_dn_p, o_dn_s.reshape(bs, v_w)], axis=0)
        outs["dn_p"].append(s_dn_p)

        d_exp = jnp.repeat(ssm_d[i], SSM_HEAD_DIM).reshape(1, inner)
        nw_ssm = ssm_norm_w[i].reshape(1, inner)
        y_p, s_ssm_p = ssd_prompt(xbc_p, p_sz, 0, g1, g2, 2 * hv, d_exp, nw_ssm, bp, seq, hs)
        xs_s = xbc_s[:, :inner]
        xt_s = jnp.transpose(xs_s.reshape(bs, hs, SSM_HEAD_DIM), (0, 2, 1))
        b_s = xbc_s[:, inner:inner + bc_w].reshape(bs, SSM_GROUPS, SSM_D_STATE)
        c_s = xbc_s[:, inner + bc_w:].reshape(bs, SSM_GROUPS, SSM_D_STATE)
        y_s, ssm_s_all = ssd_step(xt_s, xs_s.reshape(bs, 1, inner), b_s, c_s, p_sz[tp:].reshape(bs, 1, inner),
                                  g1[tp:].reshape(bs, 1, LANES), g2[tp:].reshape(bs, 1, LANES),
                                  d_exp, nw_ssm, state_ssm, i, ssm_s_all)
        y_ssm = jnp.concatenate([y_p, y_s.reshape(bs, inner)], axis=0)
        outs["ssm_p"].append(s_ssm_p)

        y_a = mm(o_dn, dn_w_out, i)
        y_b = mm(y_ssm, ssm_w_out, i)
        merged = gate_merge(p_gate, 0, y_a, y_b)
        h_f32, h_bf = ln_res(h_f32, [mm(merged, w_o, i)], ln1_g[i], ln1_b[i], alpha)

        mk = mm(mem_bf, xa_wk, i)
        mv = mm(mem_bf, xa_wv, i)
        outs["mk_p"].append(mk.reshape(bp, n_mem, MEM_HEADS, hd))
        outs["mv_p"].append(mv.reshape(bp, n_mem, MEM_HEADS, hd))
        q = mm(h_bf, xa_wq, i, out_dtype=BF16)
        a_p = attention(q[:tp].reshape(bp, seq, d), mk.reshape(1, bp, n_mem, d), mv.reshape(1, bp, n_mem, d), 0)
        a_s = attention(q[tp:].reshape(bs, 1, d), mem_k_cache, mem_v_cache, i)
        att = jnp.concatenate([a_p.reshape(tp, d), a_s.reshape(bs, d)], axis=0)
        h_f32, h_bf = ln_res(h_f32, [mm(att, xa_wo, i)], ln2_g[i], ln2_b[i], alpha)

        gate_t = router(h_f32, jnp.transpose(router_w[i]), router_bias[i].reshape(ne, 1))
        gate = _pad_lanes(jnp.transpose(gate_t))
        routed = moe_dense(h_bf, gate, e_wg, e_wu, e_wd, i)
        shared = moe_dense(h_bf, ones_gate, s_wg, s_wu, s_wd, i)
        h_f32, h_bf = ln_res(h_f32, [routed, shared], ln3_g[i], ln3_b[i], alpha)

    st = lambda k: jnp.stack(outs[k])
    return (h_f32[:tp].reshape(bp, seq, d).astype(x_prompt.dtype),
            h_f32[tp:].reshape(bs, 1, d).astype(x_sample.dtype),
            st("dn_p"), st("dnc_p"), st("ssm_p"), st("ssmc_p"), st("mk_p"), st("mv_p"),
            dn_s_all, st("dnc_s"), ssm_s_all, st("ssmc_s"))
```

```python
import functools

import jax
import jax.numpy as jnp
from jax import lax
from jax.experimental import pallas as pl
from jax.experimental.pallas import tpu as pltpu

F32 = jnp.float32
BF16 = jnp.bfloat16

CONV_WIDTH = 4
CHUNK = 64
DN_HEAD = 128
SSM_HEAD_DIM = 64
SSM_GROUPS = 8
SSM_D_STATE = 128
MEM_HEADS = 4
N_EXPERT_GROUPS = 8
TOPK_GROUPS = 4
TOP_K = 8
ROUTED_SCALE = 2.5
LN_EPS = 1e-5
RMS_EPS = 1e-6
L2_EPS = 1e-6
LANES = 128
VMEM_LIMIT = 56 * 1024 * 1024


def _pick(n, cands):
    for c in cands:
        if n % c == 0:
            return c
    raise ValueError(f"no tile for {n} in {cands}")


def _params(*sem):
    return pltpu.CompilerParams(dimension_semantics=sem, vmem_limit_bytes=VMEM_LIMIT)


def _sigmoid(x):
    return 1.0 / (1.0 + jnp.exp(-x))


def _silu(x):
    return x * _sigmoid(x)


def _softplus(x):
    return jnp.maximum(x, 0.0) + jnp.log1p(jnp.exp(-jnp.abs(x)))


def _dot(a, b):
    return jnp.dot(a.astype(BF16), b.astype(BF16), preferred_element_type=F32)


def _dot_nt(a, b):
    return lax.dot_general(a.astype(BF16), b.astype(BF16), (((1,), (1,)), ((), ())),
                           preferred_element_type=F32)


def _dot_tn(a, b):
    return lax.dot_general(a.astype(BF16), b.astype(BF16), (((0,), (0,)), ((), ())),
                           preferred_element_type=F32)


def _split(x):
    hi = x.astype(BF16)
    lo = (x - hi.astype(F32)).astype(BF16)
    return hi, lo


def _dot_exact_lhs(a01, b):
    hi, lo = _split(b)
    a = a01.astype(BF16)
    return (jnp.dot(a, hi, preferred_element_type=F32) + jnp.dot(a, lo, preferred_element_type=F32))


def _tri(c, kind):
    r = lax.broadcasted_iota(jnp.int32, (c, c), 0)
    q = lax.broadcasted_iota(jnp.int32, (c, c), 1)
    if kind == "lower":
        return r >= q
    if kind == "strict":
        return r > q
    if kind == "upper":
        return r <= q
    raise ValueError(kind)


CAST_ROWS = 256


def _mm_kernel(x_ref, w_ref, *rest, lane_off):
    if lane_off:
        w2_ref, o_ref, wb_ref = rest
    else:
        o_ref, wb_ref = rest
    tn = wb_ref.shape[1]

    @pl.when(pl.program_id(1) == 0)
    def _():
        for r in range(0, w_ref.shape[0], CAST_ROWS):
            rows = slice(r, r + CAST_ROWS)
            if lane_off:
                wide = jnp.concatenate([w_ref[rows, :], w2_ref[rows, :]], axis=1)
                wb_ref[rows, :] = wide[:, lane_off:lane_off + tn].astype(BF16)
            else:
                wb_ref[rows, :] = w_ref[rows, :].astype(BF16)

    o_ref[...] = jnp.dot(x_ref[...], wb_ref[...], preferred_element_type=F32).astype(o_ref.dtype)


def mm(x, w_stack, layer, col0=0, n_cols=None, out_dtype=F32):
    m, k = x.shape
    n_total = w_stack.shape[2]
    n = n_total - col0 if n_cols is None else n_cols
    tm = _pick(m, (640, 512, 256, 128, 64, 32, 16, 8))
    tn = _pick(n, (1024, 512, 256, 128) if k <= 2048 else (512, 256, 128))
    lane_off = col0 % tn
    assert lane_off <= LANES and lane_off % 64 == 0
    base = col0 // tn
    in_specs = [pl.BlockSpec((tm, k), lambda j, i: (i, 0)),
                pl.BlockSpec((None, k, tn), lambda j, i: (layer, 0, base + j))]
    args = [x, w_stack]
    if lane_off:
        per = tn // LANES
        in_specs.append(pl.BlockSpec((None, k, LANES), lambda j, i: (layer, 0, (base + j + 1) * per)))
        args.append(w_stack)
    return pl.pallas_call(
        functools.partial(_mm_kernel, lane_off=lane_off),
        out_shape=jax.ShapeDtypeStruct((m, n), out_dtype),
        grid=(n // tn, m // tm),
        in_specs=in_specs,
        out_specs=pl.BlockSpec((tm, tn), lambda j, i: (i, j)),
        scratch_shapes=[pltpu.VMEM((k, tn), BF16)],
        compiler_params=_params("parallel", "arbitrary"),
        name="mm",
    )(*args)


def _ln_kernel(res_ref, *rest, alpha):
    *y_refs, g_ref, b_ref, of_ref, ob_ref = rest
    y = y_refs[0][...]
    for y_ref in y_refs[1:]:
        y = y + y_ref[...]
    x = alpha * res_ref[...] + y
    mu = jnp.mean(x, axis=-1, keepdims=True)
    xc = x - mu
    var = jnp.mean(xc * xc, axis=-1, keepdims=True)
    out = xc * lax.rsqrt(var + LN_EPS) * g_ref[...] + b_ref[...]
    of_ref[...] = out
    ob_ref[...] = out.astype(BF16)


def ln_res(res, ys, g, b, alpha):
    t, d = res.shape
    tm = _pick(t, (320, 256, 128, 64, 32, 16, 8))
    row = pl.BlockSpec((tm, d), lambda i: (i, 0))
    vec = pl.BlockSpec((1, d), lambda i: (0, 0))
    return pl.pallas_call(
        functools.partial(_ln_kernel, alpha=alpha),
        out_shape=(jax.ShapeDtypeStruct((t, d), F32), jax.ShapeDtypeStruct((t, d), BF16)),
        grid=(t // tm,),
        in_specs=[row] * (1 + len(ys)) + [vec, vec],
        out_specs=(row, row),
        compiler_params=_params("parallel"),
        name="ln_res",
    )(res, *ys, g.reshape(1, d), b.reshape(1, d))


def _merge_kernel(ga_ref, gb_ref, ya_ref, yb_ref, o_ref):
    o = _sigmoid(ga_ref[...]) * ya_ref[...] + _sigmoid(gb_ref[...]) * yb_ref[...]
    o_ref[...] = o.astype(o_ref.dtype)


def gate_merge(proj, gate_off, ya, yb):
    t, d = ya.shape
    tm = _pick(t, (320, 256, 128, 64, 32, 16, 8))
    tc = 512
    nb = d // tc
    off = gate_off // tc
    return pl.pallas_call(
        _merge_kernel,
        out_shape=jax.ShapeDtypeStruct((t, d), BF16),
        grid=(t // tm, nb),
        in_specs=[pl.BlockSpec((tm, tc), lambda i, j: (i, off + j)),
                  pl.BlockSpec((tm, tc), lambda i, j: (i, off + nb + j)),
                  pl.BlockSpec((tm, tc), lambda i, j: (i, j)),
                  pl.BlockSpec((tm, tc), lambda i, j: (i, j))],
        out_specs=pl.BlockSpec((tm, tc), lambda i, j: (i, j)),
        compiler_params=_params("parallel", "parallel"),
        name="gate_merge",
    )(proj, proj, ya, yb)


def _gates_kernel(pa_ref, pb_ref, bias_ref, alog_ref, o1_ref, o2_ref, *, n_beta, n_g):
    lane = lax.broadcasted_iota(jnp.int32, pa_ref.shape, 1)
    p = jnp.where(lane < n_beta + n_g, pa_ref[...], pb_ref[...])
    sp = _softplus(p + bias_ref[...])
    neg_a = -jnp.exp(alog_ref[...])
    o1_ref[...] = jnp.where(lane < n_beta, _sigmoid(p), jnp.where(lane < n_beta + n_g, neg_a * sp, sp))
    o2_ref[...] = neg_a * sp


def gates(p_ba, p_dt, bias_row, alog_row, n_beta, n_g):
    t, w = p_ba.shape
    tm = _pick(t, (640, 512, 256, 128, 64, 32, 16, 8))
    row = pl.BlockSpec((tm, w), lambda i: (i, 0))
    vec = pl.BlockSpec((1, w), lambda i: (0, 0))
    return pl.pallas_call(
        functools.partial(_gates_kernel, n_beta=n_beta, n_g=n_g),
        out_shape=(jax.ShapeDtypeStruct((t, w), F32), jax.ShapeDtypeStruct((t, w), F32)),
        grid=(t // tm,),
        in_specs=[row, row, vec, vec],
        out_specs=(row, row),
        compiler_params=_params("parallel"),
        name="gates",
    )(p_ba, p_dt, bias_row, alog_row)


def _head_l2norm(y, scale_row):
    parts = []
    for g in range(y.shape[1] // DN_HEAD):
        seg = y[:, g * DN_HEAD:(g + 1) * DN_HEAD]
        ss = jnp.sum(seg * seg, axis=-1, keepdims=True)
        parts.append(seg * lax.rsqrt(ss + L2_EPS))
    return jnp.concatenate(parts, axis=1) * scale_row


def _conv_prompt_kernel(x_ref, w_ref, b_ref, s_ref, o_ref, *, norm):
    x = x_ref[...]
    row = lax.broadcasted_iota(jnp.int32, x.shape, 0)
    acc = x * w_ref[CONV_WIDTH - 1:CONV_WIDTH, :]
    for j in range(1, CONV_WIDTH):
        xs = jnp.where(row >= j, pltpu.roll(x, j, axis=0), 0.0)
        acc = acc + xs * w_ref[CONV_WIDTH - 1 - j:CONV_WIDTH - j, :]
    y = _silu(acc + b_ref[...])
    if norm:
        y = _head_l2norm(y, s_ref[...])
    o_ref[...] = y


def conv_prompt(proj, col_off, width, seq, batch, w, bias, scale, norm):
    tc = 256
    off = col_off // tc
    vec = lambda r: pl.BlockSpec((r, tc), lambda b, j: (0, j))
    return pl.pallas_call(
        functools.partial(_conv_prompt_kernel, norm=norm),
        out_shape=jax.ShapeDtypeStruct((batch * seq, width), F32),
        grid=(batch, width // tc),
        in_specs=[pl.BlockSpec((seq, tc), lambda b, j: (b, off + j)), vec(CONV_WIDTH), vec(1), vec(1)],
        out_specs=pl.BlockSpec((seq, tc), lambda b, j: (b, j)),
        compiler_params=_params("parallel", "parallel"),
        name="conv_prompt",
    )(proj, w, bias, scale)


def _conv_step_kernel(x_ref, buf_ref, w_ref, b_ref, s_ref, o_ref, *, norm):
    acc = x_ref[...] * w_ref[CONV_WIDTH - 1:CONV_WIDTH, :]
    for j in range(CONV_WIDTH - 1):
        acc = acc + buf_ref[j] * w_ref[j:j + 1, :]
    y = _silu(acc + b_ref[...])
    if norm:
        y = _head_l2norm(y, s_ref[...])
    o_ref[...] = y


def conv_step(proj, row_off, nrows, col_off, width, buf_t, w, bias, scale, norm):
    tc = 512
    off = col_off // tc
    rb = row_off // nrows
    vec = lambda r: pl.BlockSpec((r, tc), lambda j: (0, j))
    return pl.pallas_call(
        functools.partial(_conv_step_kernel, norm=norm),
        out_shape=jax.ShapeDtypeStruct((nrows, width), F32),
        grid=(width // tc,),
        in_specs=[pl.BlockSpec((nrows, tc), lambda j: (rb, off + j)),
                  pl.BlockSpec((CONV_WIDTH - 1, nrows, tc), lambda j: (0, 0, j)),
                  vec(CONV_WIDTH), vec(1), vec(1)],
        out_specs=pl.BlockSpec((nrows, tc), lambda j: (0, j)),
        compiler_params=_params("parallel"),
        name="conv_step",
    )(proj, buf_t, w, bias, scale)


def _gated_rms(o, nw_row, z):
    ms = jnp.mean(o * o, axis=-1, keepdims=True)
    return o * lax.rsqrt(ms + RMS_EPS) * nw_row * _silu(z)


def _delta_prompt_kernel(q_ref, k_ref, v_ref, z_ref, g_ref, nw_ref, o_ref, s_ref, *, n_pairs, n_vheads):
    c = q_ref.shape[0]
    c2 = 2 * c
    H = DN_HEAD
    n = pl.program_id(1)

    @pl.when(n == 0)
    def _():
        s_ref[...] = jnp.zeros_like(s_ref)

    row = lax.broadcasted_iota(jnp.int32, (c2, c2), 0)
    col = lax.broadcasted_iota(jnp.int32, (c2, c2), 1)
    same = (row >= c) == (col >= c)
    low = same & (row >= col)
    strict = same & (row > col)
    eye = jnp.where(row == col, 1.0, 0.0).astype(F32)
    top = lax.broadcasted_iota(jnp.int32, (c2, H), 0) < c
    top1 = lax.broadcasted_iota(jnp.int32, (c2, 1), 0) < c
    top_s = lax.broadcasted_iota(jnp.int32, (2 * H, 1), 0) < H

    gblk = g_ref[...]
    cum = _dot_exact_lhs(jnp.where(_tri(c, "lower"), 1.0, 0.0), gblk)
    stack = lambda a, i, j: jnp.concatenate([a[:, i:i + 1], a[:, j:j + 1]], axis=0)
    nw = nw_ref[...]

    P = range(n_pairs)
    beta = [stack(gblk, 2 * p, 2 * p + 1) for p in P]
    gc = [stack(cum, n_vheads + 2 * p, n_vheads + 2 * p + 1) for p in P]
    k2 = [jnp.concatenate([k_ref[:, p * H:(p + 1) * H]] * 2, axis=0) for p in P]
    q2 = [jnp.concatenate([q_ref[:, p * H:(p + 1) * H]] * 2, axis=0) for p in P]
    v2 = [jnp.concatenate([v_ref[:, 2 * p * H:(2 * p + 1) * H], v_ref[:, (2 * p + 1) * H:(2 * p + 2) * H]], axis=0)
          for p in P]
    kb = [k2[p] * beta[p] for p in P]
    egc = [jnp.exp(gc[p]) for p in P]

    decay = []
    for p in P:
        x = jnp.broadcast_to(gc[p], (c2, c2))
        decay.append(jnp.where(low, jnp.exp(jnp.where(low, x - x.T, 0.0)), 0.0))

    m1 = [_dot_nt(jnp.concatenate([kb[p], q2[p]], axis=0), k2[p]) for p in P]
    pw = [jnp.where(strict, -m1[p][:c2] * decay[p], 0.0) for p in P]
    attn = [m1[p][c2:] * decay[p] for p in P]
    inv = [eye + pw[p] for p in P]
    m = 2
    while m < c:
        pw = [_dot(pw[p], pw[p]) for p in P]
        inv = [inv[p] + _dot(inv[p], pw[p]) for p in P]
        m *= 2
    sol = [_dot(inv[p], jnp.concatenate([v2[p] * beta[p], kb[p] * egc[p]], axis=1)) for p in P]

    def blockdiag(a):
        return jnp.concatenate([jnp.where(top, a, 0.0), jnp.where(top, 0.0, a)], axis=1)

    s2 = [jnp.concatenate([s_ref[2 * p], s_ref[2 * p + 1]], axis=0) for p in P]
    m2 = [_dot(jnp.concatenate([blockdiag(sol[p][:, H:]), blockdiag(q2[p] * egc[p])], axis=0), s2[p]) for p in P]
    v_new = [sol[p][:, :H] - m2[p][:c2] for p in P]
    o = [m2[p][c2:] + _dot(attn[p], v_new[p]) for p in P]
    for p in P:
        g_last = jnp.where(top1, gc[p][c - 1:c, :], gc[p][c2 - 1:c2, :])
        kd = k2[p] * jnp.exp(g_last - gc[p])
        es = jnp.where(top_s, jnp.exp(gc[p][c - 1:c, :]), jnp.exp(gc[p][c2 - 1:c2, :]))
        s_new = s2[p] * es + _dot_tn(blockdiag(kd), v_new[p])
        s_ref[2 * p] = s_new[:H]
        s_ref[2 * p + 1] = s_new[H:]
        z2 = jnp.concatenate([z_ref[:, 2 * p * H:(2 * p + 1) * H], z_ref[:, (2 * p + 1) * H:(2 * p + 2) * H]], axis=0)
        out = _gated_rms(o[p], nw, z2).astype(o_ref.dtype)
        o_ref[:, 2 * p * H:(2 * p + 1) * H] = out[:c]
        o_ref[:, (2 * p + 1) * H:(2 * p + 2) * H] = out[c:]


def delta_prompt(qk, v, proj, z_off, g1, norm_w, batch, seq, n_vheads):
    c = CHUNK
    nc = seq // c
    hk = n_vheads // 2
    qw = hk * DN_HEAD
    vw = n_vheads * DN_HEAD
    zb = z_off // vw
    rowblk = lambda b, n: b * nc + n
    return pl.pallas_call(
        functools.partial(_delta_prompt_kernel, n_pairs=hk, n_vheads=n_vheads),
        out_shape=(jax.ShapeDtypeStruct((batch * seq, vw), BF16),
                   jax.ShapeDtypeStruct((batch, n_vheads, DN_HEAD, DN_HEAD), F32)),
        grid=(batch, nc),
        in_specs=[pl.BlockSpec((c, qw), lambda b, n: (rowblk(b, n), 0)),
                  pl.BlockSpec((c, qw), lambda b, n: (rowblk(b, n), 1)),
                  pl.BlockSpec((c, vw), lambda b, n: (rowblk(b, n), 0)),
                  pl.BlockSpec((c, vw), lambda b, n: (rowblk(b, n), zb)),
                  pl.BlockSpec((c, LANES), lambda b, n: (rowblk(b, n), 0)),
                  pl.BlockSpec((1, DN_HEAD), lambda b, n: (0, 0))],
        out_specs=(pl.BlockSpec((c, vw), lambda b, n: (rowblk(b, n), 0)),
                   pl.BlockSpec((None, n_vheads, DN_HEAD, DN_HEAD), lambda b, n: (b, 0, 0, 0))),
        compiler_params=_params("parallel", "arbitrary"),
        name="delta_prompt",
    )(qk, qk, v, proj, g1, norm_w)


def _stacked_state_call(kern, layer, state, prev_out, in_specs, args, o_shape, o_spec, grid, name):
    rest = state.shape[2:]
    zeros = (0,) * len(rest)
    st_in = pl.BlockSpec((None, 1) + rest, lambda i: (layer, i) + zeros)
    in_specs = list(in_specs) + [st_in]
    args = list(args) + [state]
    aliases = {}
    if prev_out is None:
        st_out = pl.BlockSpec((state.shape[0], 1) + rest, lambda i: (0, i) + zeros)
        kern = functools.partial(kern, slot=layer)
    else:
        st_out = pl.BlockSpec((1, 1) + rest, lambda i: (layer, i) + zeros)
        kern = functools.partial(kern, slot=0)
        in_specs.append(pl.BlockSpec(memory_space=pl.ANY))
        args.append(prev_out)
        aliases = {len(args) - 1: 1}
    return pl.pallas_call(
        kern,
        out_shape=(o_shape, jax.ShapeDtypeStruct(state.shape, F32)),
        grid=grid,
        in_specs=in_specs,
        out_specs=(o_spec, st_out),
        input_output_aliases=aliases,
        compiler_params=_params("parallel"),
        name=name,
    )(*args)


def _own_slot(so_ref, slot):
    for j in range(so_ref.shape[0]):
        if j != slot:
            so_ref[j] = jnp.zeros(so_ref.shape[1:], so_ref.dtype)
    return so_ref.at[slot]


def _delta_step_kernel(qt_ref, kt_ref, v_ref, z_ref, gt_ref, nw_ref, s_ref, *rest, n_vheads, slot):
    o_ref = rest[-2]
    so_ref = _own_slot(rest[-1], slot)
    qt = qt_ref[0]
    kt = kt_ref[0]
    gt = gt_ref[0]
    for h in range(n_vheads):
        kh = h // 2
        qc = qt[:, kh:kh + 1]
        kc = kt[:, kh:kh + 1]
        v = v_ref[0, :, h * DN_HEAD:(h + 1) * DN_HEAD]
        beta = gt[:, h:h + 1]
        eg = jnp.exp(gt[:, n_vheads + h:n_vheads + h + 1])
        s = s_ref[0, h]
        ks = jnp.sum(kc * s, axis=0, keepdims=True)
        qs = jnp.sum(qc * s, axis=0, keepdims=True)
        qk = jnp.sum(qc * kc, axis=0, keepdims=True)
        v_new = beta * v - (beta * eg) * ks
        o = eg * qs + qk * v_new
        so_ref[0, h] = s * eg + kc * v_new
        z = z_ref[0, :, h * DN_HEAD:(h + 1) * DN_HEAD]
        o_ref[0, :, h * DN_HEAD:(h + 1) * DN_HEAD] = _gated_rms(o, nw_ref[...], z).astype(o_ref.dtype)


def delta_step(qt, kt, v3, z3, g3, norm_w, state, layer, prev_out):
    nb, n_vheads = state.shape[1], state.shape[2]
    hk = qt.shape[2]
    vw = n_vheads * DN_HEAD
    blk3 = lambda a, b_: pl.BlockSpec((1, a, b_), lambda i: (i, 0, 0))
    return _stacked_state_call(
        functools.partial(_delta_step_kernel, n_vheads=n_vheads), layer, state, prev_out,
        [blk3(DN_HEAD, hk), blk3(DN_HEAD, hk), blk3(1, vw), blk3(1, vw), blk3(1, LANES),
         pl.BlockSpec((1, DN_HEAD), lambda i: (0, 0))],
        [qt, kt, v3, z3, g3, norm_w],
        jax.ShapeDtypeStruct((nb, 1, vw), BF16), blk3(1, vw), (nb,), "delta_step")


def _ssd_finish(y, xs, z, d_row, nw_row):
    y = (y + d_row * xs) * _silu(z)
    ms = jnp.mean(y * y, axis=-1, keepdims=True)
    return y * lax.rsqrt(ms + RMS_EPS) * nw_row


def _ssd_prompt_kernel(x_ref, b_ref, c_ref, z_ref, g1_ref, g2_ref, d_ref, nw_ref, o_ref, s_ref, *, n_heads, lane0):
    c = x_ref.shape[0]
    c2 = 2 * c
    P = SSM_HEAD_DIM
    N = SSM_D_STATE
    R = n_heads // SSM_GROUPS
    gw = R * P
    n = pl.program_id(1)

    @pl.when(n == 0)
    def _():
        s_ref[...] = jnp.zeros_like(s_ref)

    dtm = g1_ref[...]
    cum = _dot_exact_lhs(jnp.where(_tri(c, "lower"), 1.0, 0.0), g2_ref[...])
    left = lax.broadcasted_iota(jnp.int32, (c, 2 * P), 1) < P
    rowi = lax.broadcasted_iota(jnp.int32, (c, c2), 0)
    colj = lax.broadcasted_iota(jnp.int32, (c, c2), 1)
    first = colj < c
    causal = rowi >= jnp.where(first, colj, colj - c)
    top_s = lax.broadcasted_iota(jnp.int32, (2 * P, 1), 0) < P

    G = range(SSM_GROUPS)
    bm = [b_ref[:, g * N:(g + 1) * N] for g in G]
    cm = [c_ref[:, g * N:(g + 1) * N] for g in G]
    cb2 = [_dot_nt(cm[g], jnp.concatenate([bm[g], bm[g]], axis=0)) for g in G]
    y2 = [_dot_nt(cm[g], s_ref[g * R:(g + 1) * R].reshape(gw, N)) for g in G]

    pairs = [(g, pr) for g in G for pr in range(R // 2)]
    col = lambda a, h: a[:, lane0 + h:lane0 + h + 1]
    cc, xdt, y, upd = {}, {}, {}, {}
    for g, pr in pairs:
        h0 = g * R + 2 * pr
        cc[g, pr] = (col(cum, h0), col(cum, h0 + 1))
        dt2 = jnp.where(left, col(dtm, h0), col(dtm, h0 + 1))
        xdt[g, pr] = x_ref[:, h0 * P:(h0 + 2) * P] * dt2
    for g, pr in pairs:
        c0, c1 = cc[g, pr]
        x = jnp.broadcast_to(jnp.concatenate([c0, c1], axis=0), (c2, c2))
        cr2 = x.T[:c, :]
        seg = jnp.where(causal, jnp.exp(jnp.where(causal, jnp.where(first, c0, c1) - cr2, 0.0)), 0.0)
        xd = xdt[g, pr]
        bd = jnp.concatenate([jnp.where(left, xd, 0.0), jnp.where(left, 0.0, xd)], axis=0)
        ecum = jnp.where(left, jnp.exp(c0), jnp.exp(c1))
        y[g, pr] = _dot(cb2[g] * seg, bd) + y2[g][:, pr * 2 * P:(pr + 1) * 2 * P] * ecum
    for g, pr in pairs:
        c0, c1 = cc[g, pr]
        l0, l1 = c0[c - 1:c, :], c1[c - 1:c, :]
        e2 = jnp.where(left, jnp.exp(l0 - c0), jnp.exp(l1 - c1))
        upd[g, pr] = (_dot_tn(xdt[g, pr] * e2, bm[g]), jnp.where(top_s, jnp.exp(l0), jnp.exp(l1)))
    for g, pr in pairs:
        h0 = g * R + 2 * pr
        u, el = upd[g, pr]
        s_new = s_ref[h0:h0 + 2].reshape(2 * P, N) * el + u
        s_ref[h0] = s_new[:P]
        s_ref[h0 + 1] = s_new[P:]
    for g in G:
        sl = slice(g * gw, (g + 1) * gw)
        yg = jnp.concatenate([y[g, pr] for pr in range(R // 2)], axis=1)
        o_ref[:, sl] = _ssd_finish(yg, x_ref[:, sl], z_ref[:, sl], d_ref[:, sl], nw_ref[:, sl]).astype(o_ref.dtype)


def ssd_prompt(xbc, proj, z_off, g1, g2, lane0, d_exp, norm_w, batch, seq, n_heads):
    c = CHUNK
    nc = seq // c
    inner = n_heads * SSM_HEAD_DIM
    bc_w = SSM_GROUPS * SSM_D_STATE
    b_off = inner // bc_w
    zb = z_off // inner
    rowblk = lambda b, n: b * nc + n
    vec = pl.BlockSpec((1, inner), lambda b, n: (0, 0))
    gate = pl.BlockSpec((c, LANES), lambda b, n: (rowblk(b, n), 0))
    return pl.pallas_call(
        functools.partial(_ssd_prompt_kernel, n_heads=n_heads, lane0=lane0),
        out_shape=(jax.ShapeDtypeStruct((batch * seq, inner), BF16),
                   jax.ShapeDtypeStruct((batch, n_heads, SSM_HEAD_DIM, SSM_D_STATE), F32)),
        grid=(batch, nc),
        in_specs=[pl.BlockSpec((c, inner), lambda b, n: (rowblk(b, n), 0)),
                  pl.BlockSpec((c, bc_w), lambda b, n: (rowblk(b, n), b_off)),
                  pl.BlockSpec((c, bc_w), lambda b, n: (rowblk(b, n), b_off + 1)),
                  pl.BlockSpec((c, inner), lambda b, n: (rowblk(b, n), zb)),
                  gate, gate, vec, vec],
        out_specs=(pl.BlockSpec((c, inner), lambda b, n: (rowblk(b, n), 0)),
                   pl.BlockSpec((None, n_heads, SSM_HEAD_DIM, SSM_D_STATE), lambda b, n: (b, 0, 0, 0))),
        compiler_params=_params("parallel", "arbitrary"),
        name="ssd_prompt",
    )(xbc, xbc, xbc, proj, g1, g2, d_exp, norm_w)


def _ssd_step_kernel(xt_ref, xr_ref, b_ref, c_ref, z_ref, g1_ref, g2_ref, d_ref, nw_ref, s_ref,
                     *rest, n_heads, slot):
    o_ref = rest[-2]
    so_ref = _own_slot(rest[-1], slot)
    R = n_heads // SSM_GROUPS
    P = SSM_HEAD_DIM
    gw = R * P
    xt = xt_ref[0]
    g1 = g1_ref[0]
    g2 = g2_ref[0]
    off = LANES - n_heads
    for g in range(SSM_GROUPS):
        brow = b_ref[0, g:g + 1, :]
        for r in range(R):
            h = g * R + r
            dt = g1[:, off + h:off + h + 1]
            eda = jnp.exp(g2[:, off + h:off + h + 1])
            so_ref[0, h] = s_ref[0, h] * eda + (xt[:, h:h + 1] * dt) * brow
        s_new = so_ref[0, g * R:(g + 1) * R].reshape(gw, SSM_D_STATE)
        c8 = jnp.broadcast_to(c_ref[0, g:g + 1, :], (8, SSM_D_STATE))
        y = _dot_nt(c8, s_new)[0:1, :]
        sl = slice(g * gw, (g + 1) * gw)
        o_ref[0, :, sl] = _ssd_finish(y, xr_ref[0, :, sl], z_ref[0, :, sl], d_ref[:, sl],
                                      nw_ref[:, sl]).astype(o_ref.dtype)


def ssd_step(xt, x3, b3, c3, z3, g1, g2, d_exp, norm_w, state, layer, prev_out):
    nb, n_heads = state.shape[1], state.shape[2]
    inner = n_heads * SSM_HEAD_DIM
    blk3 = lambda a, b_: pl.BlockSpec((1, a, b_), lambda i: (i, 0, 0))
    vec = pl.BlockSpec((1, inner), lambda i: (0, 0))
    return _stacked_state_call(
        functools.partial(_ssd_step_kernel, n_heads=n_heads), layer, state, prev_out,
        [blk3(SSM_HEAD_DIM, n_heads), blk3(1, inner), blk3(SSM_GROUPS, SSM_D_STATE),
         blk3(SSM_GROUPS, SSM_D_STATE), blk3(1, inner), blk3(1, LANES), blk3(1, LANES), vec, vec],
        [xt, x3, b3, c3, z3, g1, g2, d_exp, norm_w],
        jax.ShapeDtypeStruct((nb, 1, inner), BF16), blk3(1, inner), (nb,), "ssd_step")


def _attn_kernel(q_ref, k_ref, v_ref, o_ref):
    d = q_ref.shape[2]
    hd = d // MEM_HEADS
    scale = hd ** -0.5
    for h in range(MEM_HEADS):
        sl = slice(h * hd, (h + 1) * hd)
        s = _dot_nt(q_ref[0, :, sl], k_ref[0, :, sl]) * scale
        s = s - jnp.max(s, axis=-1, keepdims=True)
        p = jnp.exp(s)
        p = p / jnp.sum(p, axis=-1, keepdims=True)
        o_ref[0, :, sl] = _dot(p, v_ref[0, :, sl]).astype(o_ref.dtype)


def attention(q, k, v, layer):
    b, l, d = q.shape
    m = k.shape[2]
    tl = _pick(l, (512, 256, 128, 64, 32, 16, 8, 1))
    kv = pl.BlockSpec((None, 1, m, d), lambda i, j: (layer, i, 0, 0))
    return pl.pallas_call(
        _attn_kernel,
        out_shape=jax.ShapeDtypeStruct((b, l, d), BF16),
        grid=(b, l // tl),
        in_specs=[pl.BlockSpec((1, tl, d), lambda i, j: (i, j, 0)), kv, kv],
        out_specs=pl.BlockSpec((1, tl, d), lambda i, j: (i, j, 0)),
        compiler_params=_params("parallel", "arbitrary"),
        name="attention",
    )(q, k, v)


def _router_kernel(t_ref, w_ref, b_ref, idx_ref, w8_ref, pos_ref, cnt_ref, carry_ref):
    logits = _dot_nt(w_ref[...], t_ref[...])
    scores = _sigmoid(logits)
    sel = scores + b_ref[...]
    ne, tm = sel.shape
    epg = ne // N_EXPERT_GROUPS
    neg = -jnp.inf

    s3 = sel.reshape(N_EXPERT_GROUPS, epg, tm)
    i3 = lax.broadcasted_iota(jnp.int32, s3.shape, 1)
    m1 = jnp.max(s3, axis=1, keepdims=True)
    a1 = jnp.min(jnp.where(s3 == m1, i3, epg), axis=1, keepdims=True)
    m2 = jnp.max(jnp.where(i3 == a1, neg, s3), axis=1, keepdims=True)
    grp = (m1 + m2).reshape(N_EXPERT_GROUPS, tm)

    gi = lax.broadcasted_iota(jnp.int32, grp.shape, 0)
    gsel = jnp.zeros(grp.shape, jnp.bool_)
    for _ in range(TOPK_GROUPS):
        mx = jnp.max(grp, axis=0, keepdims=True)
        am = jnp.min(jnp.where(grp == mx, gi, N_EXPERT_GROUPS), axis=0, keepdims=True)
        hit = gi == am
        gsel = gsel | hit
        grp = jnp.where(hit, neg, grp)

    gmask = jnp.broadcast_to(gsel.reshape(N_EXPERT_GROUPS, 1, tm), s3.shape).reshape(ne, tm)
    cur = jnp.where(gmask, sel, neg)
    ei = lax.broadcasted_iota(jnp.int32, cur.shape, 0)
    chosen = jnp.zeros(cur.shape, jnp.bool_)
    rounds = []
    for _ in range(TOP_K):
        mx = jnp.max(cur, axis=0, keepdims=True)
        am = jnp.min(jnp.where(cur == mx, ei, ne), axis=0, keepdims=True)
        hit = ei == am
        rounds.append((hit, am))
        chosen = chosen | hit
        cur = jnp.where(hit, neg, cur)

    wts = jnp.where(chosen, scores, 0.0)
    wn = wts / jnp.sum(wts, axis=0, keepdims=True) * ROUTED_SCALE

    @pl.when(pl.program_id(0) == 0)
    def _():
        carry_ref[...] = jnp.zeros_like(carry_ref)

    ch = jnp.where(chosen, 1.0, 0.0)
    before = lax.broadcasted_iota(jnp.int32, (tm, tm), 0) < lax.broadcasted_iota(jnp.int32, (tm, tm), 1)
    pos = carry_ref[:, 0:1] + _dot(ch, jnp.where(before, 1.0, 0.0))
    carry_ref[...] = carry_ref[...] + jnp.sum(ch, axis=1, keepdims=True)
    cnt_ref[...] = carry_ref[...]

    pick = lambda hit, a: jnp.sum(jnp.where(hit, a, 0.0), axis=0, keepdims=True)
    idx_ref[...] = jnp.concatenate([am for _, am in rounds], axis=0)
    w8_ref[...] = jnp.concatenate([pick(hit, wn) for hit, _ in rounds], axis=0)
    pos_ref[...] = jnp.concatenate([pick(hit, pos) for hit, _ in rounds], axis=0).astype(jnp.int32)


def router(t, w_t, bias_col):
    tt, d = t.shape
    ne = w_t.shape[0]
    tm = _pick(tt, (640, 512, 256, 128))
    slot = pl.BlockSpec((TOP_K, tm), lambda i: (0, i))
    return pl.pallas_call(
        _router_kernel,
        out_shape=(jax.ShapeDtypeStruct((TOP_K, tt), jnp.int32), jax.ShapeDtypeStruct((TOP_K, tt), F32),
                   jax.ShapeDtypeStruct((TOP_K, tt), jnp.int32), jax.ShapeDtypeStruct((ne, LANES), F32)),
        grid=(tt // tm,),
        in_specs=[pl.BlockSpec((tm, d), lambda i: (i, 0)),
                  pl.BlockSpec((ne, d), lambda i: (0, 0)),
                  pl.BlockSpec((ne, 1), lambda i: (0, 0))],
        out_specs=(slot, slot, slot, pl.BlockSpec((ne, LANES), lambda i: (0, 0))),
        scratch_shapes=[pltpu.VMEM((ne, LANES), F32)],
        compiler_params=_params("arbitrary"),
        name="router",
    )(t, w_t, bias_col)


def _moe_kernel(t_ref, g_ref, wg_ref, wu_ref, wd_ref, o_ref):
    e = pl.program_id(1)

    @pl.when(e == 0)
    def _():
        o_ref[...] = jnp.zeros_like(o_ref)

    t = t_ref[...]
    gate = g_ref[...]
    lane = lax.broadcasted_iota(jnp.int32, gate.shape, 1)
    gcol = jnp.sum(jnp.where(lane == e, gate, 0.0), axis=1, keepdims=True)
    hid = _silu(jnp.dot(t, wg_ref[0], preferred_element_type=F32)) * jnp.dot(
        t, wu_ref[0], preferred_element_type=F32)
    o_ref[...] += _dot(hid * gcol, wd_ref[0])


def moe_dense(t, gate, wg, wu, wd, layer):
    tt, d = t.shape
    _, ne, _, ff = wg.shape
    tm = _pick(tt, (640, 512, 256, 128))
    return pl.pallas_call(
        _moe_kernel,
        out_shape=jax.ShapeDtypeStruct((tt, d), F32),
        grid=(tt // tm, ne),
        in_specs=[pl.BlockSpec((tm, d), lambda i, e: (i, 0)),
                  pl.BlockSpec((tm, LANES), lambda i, e: (i, 0)),
                  pl.BlockSpec((None, 1, d, ff), lambda i, e: (layer, e, 0, 0)),
                  pl.BlockSpec((None, 1, d, ff), lambda i, e: (layer, e, 0, 0)),
                  pl.BlockSpec((None, 1, ff, d), lambda i, e: (layer, e, 0, 0))],
        out_specs=pl.BlockSpec((tm, d), lambda i, e: (i, 0)),
        compiler_params=_params("parallel", "arbitrary"),
        name="moe_dense",
    )(t, gate, wg, wu, wd)


EXPERT_TILE = 256
TOKEN_TILE = 128


def _row_copy(src, s_row, dst, d_row, sem):
    return pltpu.make_async_copy(src.at[pl.ds(s_row, 1), :], dst.at[pl.ds(d_row, 1), :], sem)


def _dispatch_kernel(dest_ref, x_hbm, xs_hbm, sem):
    base = pl.program_id(0) * TOKEN_TILE

    def issue(t, carry):
        for k in range(TOP_K):
            _row_copy(x_hbm, base + t, xs_hbm, dest_ref[k, t], sem).start()
        return carry

    lax.fori_loop(0, TOKEN_TILE, issue, 0)

    def drain(t, carry):
        for k in range(TOP_K):
            _row_copy(x_hbm, 0, xs_hbm, 0, sem).wait()
        return carry

    lax.fori_loop(0, TOKEN_TILE, drain, 0)


def dispatch_rows(dest, x, n_rows):
    tt, d = x.shape
    return pl.pallas_call(
        _dispatch_kernel,
        out_shape=jax.ShapeDtypeStruct((n_rows, d), x.dtype),
        grid=(tt // TOKEN_TILE,),
        in_specs=[pl.BlockSpec((TOP_K, TOKEN_TILE), lambda i: (0, i), memory_space=pltpu.SMEM),
                  pl.BlockSpec(memory_space=pl.ANY)],
        out_specs=pl.BlockSpec(memory_space=pl.ANY),
        scratch_shapes=[pltpu.SemaphoreType.DMA],
        compiler_params=_params("arbitrary"),
        name="dispatch_rows",
    )(dest, x)


def _experts_kernel(it_ref, ie_ref, lo_ref, hi_ref, x_ref, wg_ref, wu_ref, wd_ref, y_ref,
                    wgb_ref, wub_ref, wdb_ref):
    i = pl.program_id(0)
    p = jnp.maximum(i - 1, 0)

    @pl.when((i == 0) | (ie_ref[i] != ie_ref[p]))
    def _():
        for src, dst in ((wg_ref, wgb_ref), (wu_ref, wub_ref), (wd_ref, wdb_ref)):
            for r in range(0, src.shape[0], CAST_ROWS):
                dst[r:r + CAST_ROWS, :] = src[r:r + CAST_ROWS, :].astype(BF16)

    first = (i == 0) | (it_ref[i] != it_ref[p])

    @pl.when(first)
    def _():
        y_ref[...] = jnp.zeros_like(y_ref)

    @pl.when(hi_ref[i] > lo_ref[i])
    def _():
        row = lax.broadcasted_iota(jnp.int32, (x_ref.shape[0], 1), 0)
        mine = (row >= lo_ref[i]) & (row < hi_ref[i])
        x = jnp.where(mine, x_ref[...], 0.0).astype(BF16)
        hid = _silu(jnp.dot(x, wgb_ref[...], preferred_element_type=F32)) * jnp.dot(
            x, wub_ref[...], preferred_element_type=F32)
        y_ref[...] += _dot(hid, wdb_ref[...])


def experts_sorted(item_tile, item_expert, item_lo, item_hi, xs, wg, wu, wd, layer):
    n_rows, d = xs.shape
    ff = wg.shape[3]
    te = EXPERT_TILE
    wspec = lambda a, b_: pl.BlockSpec((None, None, a, b_), lambda i, t, e, lo, hi: (layer, e[i], 0, 0))
    rows = pl.BlockSpec((te, d), lambda i, t, e, lo, hi: (t[i], 0))
    return pl.pallas_call(
        _experts_kernel,
        out_shape=jax.ShapeDtypeStruct((n_rows, d), F32),
        grid_spec=pltpu.PrefetchScalarGridSpec(
            num_scalar_prefetch=4,
            grid=(item_tile.shape[0],),
            in_specs=[rows, wspec(d, ff), wspec(d, ff), wspec(ff, d)],
            out_specs=rows,
            scratch_shapes=[pltpu.VMEM((d, ff), BF16), pltpu.VMEM((d, ff), BF16), pltpu.VMEM((ff, d), BF16)]),
        compiler_params=_params("arbitrary"),
        name="experts_sorted",
    )(item_tile, item_expert, item_lo, item_hi, xs, wg, wu, wd)


def _combine_kernel(dest_ref, w_ref, res_ref, sh_ref, g_ref, b_ref, y_hbm, of_ref, ob_ref, buf_ref, sem, *, alpha):
    def issue(t, carry):
        for k in range(TOP_K):
            _row_copy(y_hbm, dest_ref[k, t], buf_ref.at[k], t, sem).start()
        return carry

    lax.fori_loop(0, TOKEN_TILE, issue, 0)

    def drain(t, carry):
        for k in range(TOP_K):
            _row_copy(y_hbm, 0, buf_ref.at[k], 0, sem).wait()
        return carry

    lax.fori_loop(0, TOKEN_TILE, drain, 0)

    acc = sh_ref[...]
    for k in range(TOP_K):
        acc = acc + w_ref[:, k:k + 1] * buf_ref[k]
    x = alpha * res_ref[...] + acc
    mu = jnp.mean(x, axis=-1, keepdims=True)
    xc = x - mu
    var = jnp.mean(xc * xc, axis=-1, keepdims=True)
    out = xc * lax.rsqrt(var + LN_EPS) * g_ref[...] + b_ref[...]
    of_ref[...] = out
    ob_ref[...] = out.astype(BF16)


def combine_ln(dest, w_t, res, shared, y, g, b, alpha):
    tt, d = res.shape
    row = pl.BlockSpec((TOKEN_TILE, d), lambda i: (i, 0))
    vec = pl.BlockSpec((1, d), lambda i: (0, 0))
    return pl.pallas_call(
        functools.partial(_combine_kernel, alpha=alpha),
        out_shape=(jax.ShapeDtypeStruct((tt, d), F32), jax.ShapeDtypeStruct((tt, d), BF16)),
        grid=(tt // TOKEN_TILE,),
        in_specs=[pl.BlockSpec((TOP_K, TOKEN_TILE), lambda i: (0, i), memory_space=pltpu.SMEM),
                  pl.BlockSpec((TOKEN_TILE, TOP_K), lambda i: (i, 0)), row, row, vec, vec,
                  pl.BlockSpec(memory_space=pl.ANY)],
        out_specs=(row, row),
        scratch_shapes=[pltpu.VMEM((TOP_K, TOKEN_TILE, d), F32), pltpu.SemaphoreType.DMA],
        compiler_params=_params("arbitrary"),
        name="combine_ln",
    )(dest, w_t, res, shared, g.reshape(1, d), b.reshape(1, d), y)


def _pad_lanes(a, width=LANES):
    return jnp.pad(a, [(0, 0)] * (a.ndim - 1) + [(0, width - a.shape[-1])])


def kernel(x_prompt, x_sample, mem_prompt, state_dn, state_dn_conv, state_ssm, state_ssm_conv, cache_mem_k, cache_mem_v, w_in, dn_conv_w, dn_a_log, dn_dt_bias, dn_norm_w, dn_w_out, ssm_conv_w, ssm_conv_b, ssm_a_log, ssm_dt_bias, ssm_d, ssm_norm_w, ssm_w_out, w_o, ln1_g, ln1_b, xa_wq, xa_wk, xa_wv, xa_wo, ln2_g, ln2_b, router_w, router_bias, exp_w_gate, exp_w_up, exp_w_down, sh_w_gate, sh_w_up, sh_w_down, ln3_g, ln3_b):
    bp, seq, d = x_prompt.shape
    bs = x_sample.shape[0]
    depth = w_in.shape[0]
    tp = bp * seq
    n_mem = mem_prompt.shape[1]
    hv = state_dn.shape[2]
    hk = hv // 2
    qk_w = hk * DN_HEAD
    v_w = hv * DN_HEAD
    dn_conv = 2 * qk_w + v_w
    hs = state_ssm.shape[2]
    inner = hs * SSM_HEAD_DIM
    bc_w = SSM_GROUPS * SSM_D_STATE
    ssm_conv = inner + 2 * bc_w
    ne = router_w.shape[2]
    alpha = (2 * depth) ** 0.25

    o_z, o_b = dn_conv, dn_conv + v_w
    o_xbc = o_b + 2 * hv
    o_sz = o_xbc + ssm_conv
    o_dt = o_sz + inner
    o_gate = o_dt + hs
    assert o_b % LANES == 0 and 2 * hv + hs == LANES and (o_dt + hs) % LANES == 0

    h_f32 = jnp.concatenate([x_prompt.reshape(tp, d), x_sample.reshape(bs, d)], axis=0)
    h_bf = h_f32.astype(BF16)
    mem_bf = mem_prompt.reshape(bp * n_mem, d).astype(BF16)
    mem_k_cache = cache_mem_k.reshape(depth, bs, n_mem, d)
    mem_v_cache = cache_mem_v.reshape(depth, bs, n_mem, d)
    hd = d // MEM_HEADS

    s_wg, s_wu, s_wd = (w[:, None].astype(BF16) for w in (sh_w_gate, sh_w_up, sh_w_down))
    ones_gate = jnp.ones((tp + bs, LANES), F32)
    n_rows = (tp + bs) * TOP_K
    n_tiles = n_rows // EXPERT_TILE

    outs = {k: [] for k in ("dn_p", "dnc_p", "ssm_p", "ssmc_p", "mk_p", "mv_p", "dnc_s", "ssmc_s")}
    dn_s_all = None
    ssm_s_all = None

    for i in range(depth):
        p_dn = mm(h_bf, w_in, i, 0, o_b)
        p_ba = mm(h_bf, w_in, i, o_b, LANES)
        p_xbc = mm(h_bf, w_in, i, o_xbc, ssm_conv)
        p_sz = mm(h_bf, w_in, i, o_sz, inner)
        p_dt = mm(h_bf, w_in, i, o_dt + hs - LANES, LANES)
        p_gate = mm(h_bf, w_in, i, o_gate, 2 * d)

        zeros_hv = jnp.zeros((hv,), F32)
        bias_row = jnp.concatenate([zeros_hv, dn_dt_bias[i], ssm_dt_bias[i]]).reshape(1, LANES)
        alog_row = jnp.concatenate([zeros_hv, dn_a_log[i], ssm_a_log[i]]).reshape(1, LANES)
        g1, g2 = gates(p_ba, p_dt, bias_row, alog_row, hv, hv)

        dn_scale = jnp.concatenate([jnp.full((qk_w,), DN_HEAD ** -0.5, F32), jnp.ones((qk_w,), F32)]).reshape(1, -1)
        dn_zero_b = jnp.zeros((1, dn_conv), F32)
        cw = dn_conv_w[i]
        qk_p = conv_prompt(p_dn, 0, 2 * qk_w, seq, bp, cw[:, :2 * qk_w], dn_zero_b[:, :2 * qk_w], dn_scale, True)
        v_p = conv_prompt(p_dn, 2 * qk_w, v_w, seq, bp, cw[:, 2 * qk_w:], dn_zero_b[:, 2 * qk_w:],
                          dn_zero_b[:, 2 * qk_w:], False)
        scw = ssm_conv_w[i]
        scb = ssm_conv_b[i].reshape(1, ssm_conv)
        xbc_p = conv_prompt(p_xbc, 0, ssm_conv, seq, bp, scw, scb, scb, False)

        dn_buf_t = jnp.transpose(state_dn_conv[i], (1, 0, 2))
        qk_s = conv_step(p_dn, tp, bs, 0, 2 * qk_w, dn_buf_t[:, :, :2 * qk_w], cw[:, :2 * qk_w],
                         dn_zero_b[:, :2 * qk_w], dn_scale, True)
        v_s = conv_step(p_dn, tp, bs, 2 * qk_w, v_w, dn_buf_t[:, :, 2 * qk_w:], cw[:, 2 * qk_w:],
                        dn_zero_b[:, 2 * qk_w:], dn_zero_b[:, 2 * qk_w:], False)
        ssm_buf_t = jnp.transpose(state_ssm_conv[i], (1, 0, 2))
        xbc_s = conv_step(p_xbc, tp, bs, 0, ssm_conv, ssm_buf_t, scw, scb, scb, False)

        tail = seq - (CONV_WIDTH - 1)
        last_rows = lambda p, w: jnp.stack([p[b * seq + tail:(b + 1) * seq, :w] for b in range(bp)])
        outs["dnc_p"].append(last_rows(p_dn, dn_conv))
        outs["ssmc_p"].append(last_rows(p_xbc, ssm_conv))
        outs["dnc_s"].append(jnp.concatenate([state_dn_conv[i][:, 1:], p_dn[tp:, None, :dn_conv]], axis=1))
        outs["ssmc_s"].append(jnp.concatenate([state_ssm_conv[i][:, 1:], p_xbc[tp:, None]], axis=1))

        nw_dn = dn_norm_w[i].reshape(1, DN_HEAD)
        o_dn_p, s_dn_p = delta_prompt(qk_p, v_p, p_dn, o_z, g1, nw_dn, bp, seq, hv)
        qt = jnp.transpose(qk_s[:, :qk_w].reshape(bs, hk, DN_HEAD), (0, 2, 1))
        kt = jnp.transpose(qk_s[:, qk_w:].reshape(bs, hk, DN_HEAD), (0, 2, 1))
        z_dn_s = p_dn[tp:, o_z:].reshape(bs, 1, v_w)
        o_dn_s, dn_s_all = delta_step(qt, kt, v_s.reshape(bs, 1, v_w), z_dn_s, g1[tp:].reshape(bs, 1, LANES),
                                      nw_dn, state_dn, i, dn_s_all)
        o_dn = jnp.concatenate([o_dn_p, o_dn_s.reshape(bs, v_w)], axis=0)
        outs["dn_p"].append(s_dn_p)

        d_exp = jnp.repeat(ssm_d[i], SSM_HEAD_DIM).reshape(1, inner)
        nw_ssm = ssm_norm_w[i].reshape(1, inner)
        y_p, s_ssm_p = ssd_prompt(xbc_p, p_sz, 0, g1, g2, 2 * hv, d_exp, nw_ssm, bp, seq, hs)
        xs_s = xbc_s[:, :inner]
        xt_s = jnp.transpose(xs_s.reshape(bs, hs, SSM_HEAD_DIM), (0, 2, 1))
        b_s = xbc_s[:, inner:inner + bc_w].reshape(bs, SSM_GROUPS, SSM_D_STATE)
        c_s = xbc_s[:, inner + bc_w:].reshape(bs, SSM_GROUPS, SSM_D_STATE)
        y_s, ssm_s_all = ssd_step(xt_s, xs_s.reshape(bs, 1, inner), b_s, c_s, p_sz[tp:].reshape(bs, 1, inner),
                                  g1[tp:].reshape(bs, 1, LANES), g2[tp:].reshape(bs, 1, LANES),
                                  d_exp, nw_ssm, state_ssm, i, ssm_s_all)
        y_ssm = jnp.concatenate([y_p, y_s.reshape(bs, inner)], axis=0)
        outs["ssm_p"].append(s_ssm_p)

        y_a = mm(o_dn, dn_w_out, i)
        y_b = mm(y_ssm, ssm_w_out, i)
        merged = gate_merge(p_gate, 0, y_a, y_b)
        h_f32, h_bf = ln_res(h_f32, [mm(merged, w_o, i)], ln1_g[i], ln1_b[i], alpha)

        mk = mm(mem_bf, xa_wk, i)
        mv = mm(mem_bf, xa_wv, i)
        outs["mk_p"].append(mk.reshape(bp, n_mem, MEM_HEADS, hd))
        outs["mv_p"].append(mv.reshape(bp, n_mem, MEM_HEADS, hd))
        q = mm(h_bf, xa_wq, i, out_dtype=BF16)
        a_p = attention(q[:tp].reshape(bp, seq, d), mk.reshape(1, bp, n_mem, d), mv.reshape(1, bp, n_mem, d), 0)
        a_s = attention(q[tp:].reshape(bs, 1, d), mem_k_cache, mem_v_cache, i)
        att = jnp.concatenate([a_p.reshape(tp, d), a_s.reshape(bs, d)], axis=0)
        h_f32, h_bf = ln_res(h_f32, [mm(att, xa_wo, i)], ln2_g[i], ln2_b[i], alpha)

        idx8, w8, pos8, cnt = router(h_f32, jnp.transpose(router_w[i]), router_bias[i].reshape(ne, 1))
        cnt = cnt[:, 0].astype(jnp.int32)
        end = jnp.cumsum(cnt)
        dest = jnp.take(end - cnt, idx8) + pos8
        cuts = jnp.sort(jnp.concatenate([jnp.arange(n_tiles, dtype=jnp.int32) * EXPERT_TILE, end[:-1]]))
        nxt = jnp.concatenate([cuts[1:], jnp.full((1,), n_rows, jnp.int32)])
        item_tile = jnp.minimum(cuts // EXPERT_TILE, n_tiles - 1)
        item_e = jnp.minimum(jnp.searchsorted(end, cuts, side="right"), ne - 1).astype(jnp.int32)
        item_lo = cuts - item_tile * EXPERT_TILE
        item_hi = nxt - item_tile * EXPERT_TILE

        xs = dispatch_rows(dest, h_f32, n_rows)
        ys = experts_sorted(item_tile, item_e, item_lo, item_hi, xs, exp_w_gate, exp_w_up, exp_w_down, i)
        shared = moe_dense(h_bf, ones_gate, s_wg, s_wu, s_wd, i)
        h_f32, h_bf = combine_ln(dest, jnp.transpose(w8), h_f32, shared, ys, ln3_g[i], ln3_b[i], alpha)

    st = lambda k: jnp.stack(outs[k])
    return (h_f32[:tp].reshape(bp, seq, d).astype(x_prompt.dtype),
            h_f32[tp:].reshape(bs, 1, d).astype(x_sample.dtype),
            st("dn_p"), st("dnc_p"), st("ssm_p"), st("ssmc_p"), st("mk_p"), st("mv_p"),
            dn_s_all, st("dnc_s"), ssm_s_all, st("ssmc_s"))
```

```python
import functools

import jax
import jax.numpy as jnp
from jax import lax
from jax.experimental import pallas as pl
from jax.experimental.pallas import tpu as pltpu

F32 = jnp.float32
BF16 = jnp.bfloat16

CONV_WIDTH = 4
CHUNK = 64
DN_HEAD = 128
SSM_HEAD_DIM = 64
SSM_GROUPS = 8
SSM_D_STATE = 128
MEM_HEADS = 4
N_EXPERT_GROUPS = 8
TOPK_GROUPS = 4
TOP_K = 8
ROUTED_SCALE = 2.5
LN_EPS = 1e-5
RMS_EPS = 1e-6
L2_EPS = 1e-6
LANES = 128
VMEM_LIMIT = 56 * 1024 * 1024


def _pick(n, cands):
    for c in cands:
        if n % c == 0:
            return c
    raise ValueError(f"no tile for {n} in {cands}")


def _params(*sem):
    return pltpu.CompilerParams(dimension_semantics=sem, vmem_limit_bytes=VMEM_LIMIT)


def _sigmoid(x):
    return 1.0 / (1.0 + jnp.exp(-x))


def _silu(x):
    return x * _sigmoid(x)


def _softplus(x):
    return jnp.maximum(x, 0.0) + jnp.log1p(jnp.exp(-jnp.abs(x)))


def _dot(a, b):
    return jnp.dot(a.astype(BF16), b.astype(BF16), preferred_element_type=F32)


def _dot_nt(a, b):
    return lax.dot_general(a.astype(BF16), b.astype(BF16), (((1,), (1,)), ((), ())),
                           preferred_element_type=F32)


def _dot_tn(a, b):
    return lax.dot_general(a.astype(BF16), b.astype(BF16), (((0,), (0,)), ((), ())),
                           preferred_element_type=F32)


def _split(x):
    hi = x.astype(BF16)
    lo = (x - hi.astype(F32)).astype(BF16)
    return hi, lo


def _dot_exact_lhs(a01, b):
    hi, lo = _split(b)
    a = a01.astype(BF16)
    return (jnp.dot(a, hi, preferred_element_type=F32) + jnp.dot(a, lo, preferred_element_type=F32))


def _tri(c, kind):
    r = lax.broadcasted_iota(jnp.int32, (c, c), 0)
    q = lax.broadcasted_iota(jnp.int32, (c, c), 1)
    if kind == "lower":
        return r >= q
    if kind == "strict":
        return r > q
    if kind == "upper":
        return r <= q
    raise ValueError(kind)


CAST_ROWS = 256


def _mm_kernel(x_ref, w_ref, *rest, lane_off):
    if lane_off:
        w2_ref, o_ref, wb_ref = rest
    else:
        o_ref, wb_ref = rest
    tn = wb_ref.shape[1]

    @pl.when(pl.program_id(1) == 0)
    def _():
        for r in range(0, w_ref.shape[0], CAST_ROWS):
            rows = slice(r, r + CAST_ROWS)
            if lane_off:
                wide = jnp.concatenate([w_ref[rows, :], w2_ref[rows, :]], axis=1)
                wb_ref[rows, :] = wide[:, lane_off:lane_off + tn].astype(BF16)
            else:
                wb_ref[rows, :] = w_ref[rows, :].astype(BF16)

    o_ref[...] = jnp.dot(x_ref[...], wb_ref[...], preferred_element_type=F32).astype(o_ref.dtype)


def mm(x, w_stack, layer, col0=0, n_cols=None, out_dtype=F32):
    m, k = x.shape
    n_total = w_stack.shape[2]
    n = n_total - col0 if n_cols is None else n_cols
    tm = _pick(m, (640, 512, 256, 128, 64, 32, 16, 8))
    tn = _pick(n, (1024, 512, 256, 128) if k <= 2048 else (512, 256, 128))
    lane_off = col0 % tn
    assert lane_off <= LANES and lane_off % 64 == 0
    base = col0 // tn
    in_specs = [pl.BlockSpec((tm, k), lambda j, i: (i, 0)),
                pl.BlockSpec((None, k, tn), lambda j, i: (layer, 0, base + j))]
    args = [x, w_stack]
    if lane_off:
        per = tn // LANES
        in_specs.append(pl.BlockSpec((None, k, LANES), lambda j, i: (layer, 0, (base + j + 1) * per)))
        args.append(w_stack)
    return pl.pallas_call(
        functools.partial(_mm_kernel, lane_off=lane_off),
        out_shape=jax.ShapeDtypeStruct((m, n), out_dtype),
        grid=(n // tn, m // tm),
        in_specs=in_specs,
        out_specs=pl.BlockSpec((tm, tn), lambda j, i: (i, j)),
        scratch_shapes=[pltpu.VMEM((k, tn), BF16)],
        compiler_params=_params("parallel", "arbitrary"),
        name="mm",
    )(*args)


def _ln_kernel(res_ref, *rest, alpha):
    *y_refs, g_ref, b_ref, of_ref, ob_ref = rest
    y = y_refs[0][...]
    for y_ref in y_refs[1:]:
        y = y + y_ref[...]
    x = alpha * res_ref[...] + y
    mu = jnp.mean(x, axis=-1, keepdims=True)
    xc = x - mu
    var = jnp.mean(xc * xc, axis=-1, keepdims=True)
    out = xc * lax.rsqrt(var + LN_EPS) * g_ref[...] + b_ref[...]
    of_ref[...] = out
    ob_ref[...] = out.astype(BF16)


def ln_res(res, ys, g, b, alpha):
    t, d = res.shape
    tm = _pick(t, (320, 256, 128, 64, 32, 16, 8))
    row = pl.BlockSpec((tm, d), lambda i: (i, 0))
    vec = pl.BlockSpec((1, d), lambda i: (0, 0))
    return pl.pallas_call(
        functools.partial(_ln_kernel, alpha=alpha),
        out_shape=(jax.ShapeDtypeStruct((t, d), F32), jax.ShapeDtypeStruct((t, d), BF16)),
        grid=(t // tm,),
        in_specs=[row] * (1 + len(ys)) + [vec, vec],
        out_specs=(row, row),
        compiler_params=_params("parallel"),
        name="ln_res",
    )(res, *ys, g.reshape(1, d), b.reshape(1, d))


def _merge_kernel(ga_ref, gb_ref, ya_ref, yb_ref, o_ref):
    o = _sigmoid(ga_ref[...]) * ya_ref[...] + _sigmoid(gb_ref[...]) * yb_ref[...]
    o_ref[...] = o.astype(o_ref.dtype)


def gate_merge(proj, gate_off, ya, yb):
    t, d = ya.shape
    tm = _pick(t, (320, 256, 128, 64, 32, 16, 8))
    tc = 512
    nb = d // tc
    off = gate_off // tc
    return pl.pallas_call(
        _merge_kernel,
        out_shape=jax.ShapeDtypeStruct((t, d), BF16),
        grid=(t // tm, nb),
        in_specs=[pl.BlockSpec((tm, tc), lambda i, j: (i, off + j)),
                  pl.BlockSpec((tm, tc), lambda i, j: (i, off + nb + j)),
                  pl.BlockSpec((tm, tc), lambda i, j: (i, j)),
                  pl.BlockSpec((tm, tc), lambda i, j: (i, j))],
        out_specs=pl.BlockSpec((tm, tc), lambda i, j: (i, j)),
        compiler_params=_params("parallel", "parallel"),
        name="gate_merge",
    )(proj, proj, ya, yb)


def _gates_kernel(pa_ref, pb_ref, bias_ref, alog_ref, o1_ref, o2_ref, *, n_beta, n_g):
    lane = lax.broadcasted_iota(jnp.int32, pa_ref.shape, 1)
    p = jnp.where(lane < n_beta + n_g, pa_ref[...], pb_ref[...])
    sp = _softplus(p + bias_ref[...])
    neg_a = -jnp.exp(alog_ref[...])
    o1_ref[...] = jnp.where(lane < n_beta, _sigmoid(p), jnp.where(lane < n_beta + n_g, neg_a * sp, sp))
    o2_ref[...] = neg_a * sp


def gates(p_ba, p_dt, bias_row, alog_row, n_beta, n_g):
    t, w = p_ba.shape
    tm = _pick(t, (640, 512, 256, 128, 64, 32, 16, 8))
    row = pl.BlockSpec((tm, w), lambda i: (i, 0))
    vec = pl.BlockSpec((1, w), lambda i: (0, 0))
    return pl.pallas_call(
        functools.partial(_gates_kernel, n_beta=n_beta, n_g=n_g),
        out_shape=(jax.ShapeDtypeStruct((t, w), F32), jax.ShapeDtypeStruct((t, w), F32)),
        grid=(t // tm,),
        in_specs=[row, row, vec, vec],
        out_specs=(row, row),
        compiler_params=_params("parallel"),
        name="gates",
    )(p_ba, p_dt, bias_row, alog_row)


def _head_l2norm(y, scale_row):
    parts = []
    for g in range(y.shape[1] // DN_HEAD):
        seg = y[:, g * DN_HEAD:(g + 1) * DN_HEAD]
        ss = jnp.sum(seg * seg, axis=-1, keepdims=True)
        parts.append(seg * lax.rsqrt(ss + L2_EPS))
    return jnp.concatenate(parts, axis=1) * scale_row


def _conv_prompt_kernel(x_ref, w_ref, b_ref, s_ref, o_ref, *, norm):
    x = x_ref[...]
    row = lax.broadcasted_iota(jnp.int32, x.shape, 0)
    acc = x * w_ref[CONV_WIDTH - 1:CONV_WIDTH, :]
    for j in range(1, CONV_WIDTH):
        xs = jnp.where(row >= j, pltpu.roll(x, j, axis=0), 0.0)
        acc = acc + xs * w_ref[CONV_WIDTH - 1 - j:CONV_WIDTH - j, :]
    y = _silu(acc + b_ref[...])
    if norm:
        y = _head_l2norm(y, s_ref[...])
    o_ref[...] = y


def conv_prompt(proj, col_off, width, seq, batch, w, bias, scale, norm):
    tc = 256
    off = col_off // tc
    vec = lambda r: pl.BlockSpec((r, tc), lambda b, j: (0, j))
    return pl.pallas_call(
        functools.partial(_conv_prompt_kernel, norm=norm),
        out_shape=jax.ShapeDtypeStruct((batch * seq, width), F32),
        grid=(batch, width // tc),
        in_specs=[pl.BlockSpec((seq, tc), lambda b, j: (b, off + j)), vec(CONV_WIDTH), vec(1), vec(1)],
        out_specs=pl.BlockSpec((seq, tc), lambda b, j: (b, j)),
        compiler_params=_params("parallel", "parallel"),
        name="conv_prompt",
    )(proj, w, bias, scale)


def _conv_step_kernel(x_ref, buf_ref, w_ref, b_ref, s_ref, o_ref, *, norm):
    acc = x_ref[...] * w_ref[CONV_WIDTH - 1:CONV_WIDTH, :]
    for j in range(CONV_WIDTH - 1):
        acc = acc + buf_ref[j] * w_ref[j:j + 1, :]
    y = _silu(acc + b_ref[...])
    if norm:
        y = _head_l2norm(y, s_ref[...])
    o_ref[...] = y


def conv_step(proj, row_off, nrows, col_off, width, buf_t, w, bias, scale, norm):
    tc = 512
    off = col_off // tc
    rb = row_off // nrows
    vec = lambda r: pl.BlockSpec((r, tc), lambda j: (0, j))
    return pl.pallas_call(
        functools.partial(_conv_step_kernel, norm=norm),
        out_shape=jax.ShapeDtypeStruct((nrows, width), F32),
        grid=(width // tc,),
        in_specs=[pl.BlockSpec((nrows, tc), lambda j: (rb, off + j)),
                  pl.BlockSpec((CONV_WIDTH - 1, nrows, tc), lambda j: (0, 0, j)),
                  vec(CONV_WIDTH), vec(1), vec(1)],
        out_specs=pl.BlockSpec((nrows, tc), lambda j: (0, j)),
        compiler_params=_params("parallel"),
        name="conv_step",
    )(proj, buf_t, w, bias, scale)


def _gated_rms(o, nw_row, z):
    ms = jnp.mean(o * o, axis=-1, keepdims=True)
    return o * lax.rsqrt(ms + RMS_EPS) * nw_row * _silu(z)


def _delta_prompt_kernel(q_ref, k_ref, v_ref, z_ref, g_ref, nw_ref, o_ref, s_ref, *, n_pairs, n_vheads):
    c = q_ref.shape[0]
    c2 = 2 * c
    H = DN_HEAD
    n = pl.program_id(1)

    @pl.when(n == 0)
    def _():
        s_ref[...] = jnp.zeros_like(s_ref)

    row = lax.broadcasted_iota(jnp.int32, (c2, c2), 0)
    col = lax.broadcasted_iota(jnp.int32, (c2, c2), 1)
    same = (row >= c) == (col >= c)
    low = same & (row >= col)
    strict = same & (row > col)
    eye = jnp.where(row == col, 1.0, 0.0).astype(F32)
    top = lax.broadcasted_iota(jnp.int32, (c2, H), 0) < c
    top1 = lax.broadcasted_iota(jnp.int32, (c2, 1), 0) < c
    top_s = lax.broadcasted_iota(jnp.int32, (2 * H, 1), 0) < H

    gblk = g_ref[...]
    cum = _dot_exact_lhs(jnp.where(_tri(c, "lower"), 1.0, 0.0), gblk)
    stack = lambda a, i, j: jnp.concatenate([a[:, i:i + 1], a[:, j:j + 1]], axis=0)
    nw = nw_ref[...]

    P = range(n_pairs)
    beta = [stack(gblk, 2 * p, 2 * p + 1) for p in P]
    gc = [stack(cum, n_vheads + 2 * p, n_vheads + 2 * p + 1) for p in P]
    k2 = [jnp.concatenate([k_ref[:, p * H:(p + 1) * H]] * 2, axis=0) for p in P]
    q2 = [jnp.concatenate([q_ref[:, p * H:(p + 1) * H]] * 2, axis=0) for p in P]
    v2 = [jnp.concatenate([v_ref[:, 2 * p * H:(2 * p + 1) * H], v_ref[:, (2 * p + 1) * H:(2 * p + 2) * H]], axis=0)
          for p in P]
    kb = [k2[p] * beta[p] for p in P]
    egc = [jnp.exp(gc[p]) for p in P]

    decay = []
    for p in P:
        x = jnp.broadcast_to(gc[p], (c2, c2))
        decay.append(jnp.where(low, jnp.exp(jnp.where(low, x - x.T, 0.0)), 0.0))

    m1 = [_dot_nt(jnp.concatenate([kb[p], q2[p]], axis=0), k2[p]) for p in P]
    pw = [jnp.where(strict, -m1[p][:c2] * decay[p], 0.0) for p in P]
    attn = [m1[p][c2:] * decay[p] for p in P]
    inv = [eye + pw[p] for p in P]
    m = 2
    while m < c:
        pw = [_dot(pw[p], pw[p]) for p in P]
        inv = [inv[p] + _dot(inv[p], pw[p]) for p in P]
        m *= 2
    sol = [_dot(inv[p], jnp.concatenate([v2[p] * beta[p], kb[p] * egc[p]], axis=1)) for p in P]

    def blockdiag(a):
        return jnp.concatenate([jnp.where(top, a, 0.0), jnp.where(top, 0.0, a)], axis=1)

    s2 = [jnp.concatenate([s_ref[2 * p], s_ref[2 * p + 1]], axis=0) for p in P]
    m2 = [_dot(jnp.concatenate([blockdiag(sol[p][:, H:]), blockdiag(q2[p] * egc[p])], axis=0), s2[p]) for p in P]
    v_new = [sol[p][:, :H] - m2[p][:c2] for p in P]
    o = [m2[p][c2:] + _dot(attn[p], v_new[p]) for p in P]
    for p in P:
        g_last = jnp.where(top1, gc[p][c - 1:c, :], gc[p][c2 - 1:c2, :])
        kd = k2[p] * jnp.exp(g_last - gc[p])
        es = jnp.where(top_s, jnp.exp(gc[p][c - 1:c, :]), jnp.exp(gc[p][c2 - 1:c2, :]))
        s_new = s2[p] * es + _dot_tn(blockdiag(kd), v_new[p])
        s_ref[2 * p] = s_new[:H]
        s_ref[2 * p + 1] = s_new[H:]
        z2 = jnp.concatenate([z_ref[:, 2 * p * H:(2 * p + 1) * H], z_ref[:, (2 * p + 1) * H:(2 * p + 2) * H]], axis=0)
        out = _gated_rms(o[p], nw, z2).astype(o_ref.dtype)
        o_ref[:, 2 * p * H:(2 * p + 1) * H] = out[:c]
        o_ref[:, (2 * p + 1) * H:(2 * p + 2) * H] = out[c:]


def delta_prompt(qk, v, proj, z_off, g1, norm_w, batch, seq, n_vheads):
    c = CHUNK
    nc = seq // c
    hk = n_vheads // 2
    qw = hk * DN_HEAD
    vw = n_vheads * DN_HEAD
    zb = z_off // vw
    rowblk = lambda b, n: b * nc + n
    return pl.pallas_call(
        functools.partial(_delta_prompt_kernel, n_pairs=hk, n_vheads=n_vheads),
        out_shape=(jax.ShapeDtypeStruct((batch * seq, vw), BF16),
                   jax.ShapeDtypeStruct((batch, n_vheads, DN_HEAD, DN_HEAD), F32)),
        grid=(batch, nc),
        in_specs=[pl.BlockSpec((c, qw), lambda b, n: (rowblk(b, n), 0)),
                  pl.BlockSpec((c, qw), lambda b, n: (rowblk(b, n), 1)),
                  pl.BlockSpec((c, vw), lambda b, n: (rowblk(b, n), 0)),
                  pl.BlockSpec((c, vw), lambda b, n: (rowblk(b, n), zb)),
                  pl.BlockSpec((c, LANES), lambda b, n: (rowblk(b, n), 0)),
                  pl.BlockSpec((1, DN_HEAD), lambda b, n: (0, 0))],
        out_specs=(pl.BlockSpec((c, vw), lambda b, n: (rowblk(b, n), 0)),
                   pl.BlockSpec((None, n_vheads, DN_HEAD, DN_HEAD), lambda b, n: (b, 0, 0, 0))),
        compiler_params=_params("parallel", "arbitrary"),
        name="delta_prompt",
    )(qk, qk, v, proj, g1, norm_w)


def _stacked_state_call(kern, layer, state, prev_out, in_specs, args, o_shape, o_spec, grid, name):
    rest = state.shape[2:]
    zeros = (0,) * len(rest)
    st_in = pl.BlockSpec((None, 1) + rest, lambda i: (layer, i) + zeros)
    in_specs = list(in_specs) + [st_in]
    args = list(args) + [state]
    aliases = {}
    if prev_out is None:
        st_out = pl.BlockSpec((state.shape[0], 1) + rest, lambda i: (0, i) + zeros)
        kern = functools.partial(kern, slot=layer)
    else:
        st_out = pl.BlockSpec((1, 1) + rest, lambda i: (layer, i) + zeros)
        kern = functools.partial(kern, slot=0)
        in_specs.append(pl.BlockSpec(memory_space=pl.ANY))
        args.append(prev_out)
        aliases = {len(args) - 1: 1}
    return pl.pallas_call(
        kern,
        out_shape=(o_shape, jax.ShapeDtypeStruct(state.shape, F32)),
        grid=grid,
        in_specs=in_specs,
        out_specs=(o_spec, st_out),
        input_output_aliases=aliases,
        compiler_params=_params("parallel"),
        name=name,
    )(*args)


def _own_slot(so_ref, slot):
    for j in range(so_ref.shape[0]):
        if j != slot:
            so_ref[j] = jnp.zeros(so_ref.shape[1:], so_ref.dtype)
    return so_ref.at[slot]


def _delta_step_kernel(qt_ref, kt_ref, v_ref, z_ref, gt_ref, nw_ref, s_ref, *rest, n_vheads, slot):
    o_ref = rest[-2]
    so_ref = _own_slot(rest[-1], slot)
    qt = qt_ref[0]
    kt = kt_ref[0]
    gt = gt_ref[0]
    for h in range(n_vheads):
        kh = h // 2
        qc = qt[:, kh:kh + 1]
        kc = kt[:, kh:kh + 1]
        v = v_ref[0, :, h * DN_HEAD:(h + 1) * DN_HEAD]
        beta = gt[:, h:h + 1]
        eg = jnp.exp(gt[:, n_vheads + h:n_vheads + h + 1])
        s = s_ref[0, h]
        ks = jnp.sum(kc * s, axis=0, keepdims=True)
        qs = jnp.sum(qc * s, axis=0, keepdims=True)
        qk = jnp.sum(qc * kc, axis=0, keepdims=True)
        v_new = beta * v - (beta * eg) * ks
        o = eg * qs + qk * v_new
        so_ref[0, h] = s * eg + kc * v_new
        z = z_ref[0, :, h * DN_HEAD:(h + 1) * DN_HEAD]
        o_ref[0, :, h * DN_HEAD:(h + 1) * DN_HEAD] = _gated_rms(o, nw_ref[...], z).astype(o_ref.dtype)


def delta_step(qt, kt, v3, z3, g3, norm_w, state, layer, prev_out):
    nb, n_vheads = state.shape[1], state.shape[2]
    hk = qt.shape[2]
    vw = n_vheads * DN_HEAD
    blk3 = lambda a, b_: pl.BlockSpec((1, a, b_), lambda i: (i, 0, 0))
    return _stacked_state_call(
        functools.partial(_delta_step_kernel, n_vheads=n_vheads), layer, state, prev_out,
        [blk3(DN_HEAD, hk), blk3(DN_HEAD, hk), blk3(1, vw), blk3(1, vw), blk3(1, LANES),
         pl.BlockSpec((1, DN_HEAD), lambda i: (0, 0))],
        [qt, kt, v3, z3, g3, norm_w],
        jax.ShapeDtypeStruct((nb, 1, vw), BF16), blk3(1, vw), (nb,), "delta_step")


def _ssd_finish(y, xs, z, d_row, nw_row):
    y = (y + d_row * xs) * _silu(z)
    ms = jnp.mean(y * y, axis=-1, keepdims=True)
    return y * lax.rsqrt(ms + RMS_EPS) * nw_row


def _ssd_prompt_kernel(x_ref, b_ref, c_ref, z_ref, g1_ref, g2_ref, d_ref, nw_ref, o_ref, s_ref, *, n_heads, lane0):
    c = x_ref.shape[0]
    c2 = 2 * c
    P = SSM_HEAD_DIM
    N = SSM_D_STATE
    R = n_heads // SSM_GROUPS
    gw = R * P
    n = pl.program_id(1)

    @pl.when(n == 0)
    def _():
        s_ref[...] = jnp.zeros_like(s_ref)

    dtm = g1_ref[...]
    cum = _dot_exact_lhs(jnp.where(_tri(c, "lower"), 1.0, 0.0), g2_ref[...])
    left = lax.broadcasted_iota(jnp.int32, (c, 2 * P), 1) < P
    rowi = lax.broadcasted_iota(jnp.int32, (c, c2), 0)
    colj = lax.broadcasted_iota(jnp.int32, (c, c2), 1)
    first = colj < c
    causal = rowi >= jnp.where(first, colj, colj - c)
    top_s = lax.broadcasted_iota(jnp.int32, (2 * P, 1), 0) < P

    G = range(SSM_GROUPS)
    bm = [b_ref[:, g * N:(g + 1) * N] for g in G]
    cm = [c_ref[:, g * N:(g + 1) * N] for g in G]
    cb2 = [_dot_nt(cm[g], jnp.concatenate([bm[g], bm[g]], axis=0)) for g in G]
    y2 = [_dot_nt(cm[g], s_ref[g * R:(g + 1) * R].reshape(gw, N)) for g in G]

    pairs = [(g, pr) for g in G for pr in range(R // 2)]
    col = lambda a, h: a[:, lane0 + h:lane0 + h + 1]
    cc, xdt, y, upd = {}, {}, {}, {}
    for g, pr in pairs:
        h0 = g * R + 2 * pr
        cc[g, pr] = (col(cum, h0), col(cum, h0 + 1))
        dt2 = jnp.where(left, col(dtm, h0), col(dtm, h0 + 1))
        xdt[g, pr] = x_ref[:, h0 * P:(h0 + 2) * P] * dt2
    for g, pr in pairs:
        c0, c1 = cc[g, pr]
        x = jnp.broadcast_to(jnp.concatenate([c0, c1], axis=0), (c2, c2))
        cr2 = x.T[:c, :]
        seg = jnp.where(causal, jnp.exp(jnp.where(causal, jnp.where(first, c0, c1) - cr2, 0.0)), 0.0)
        xd = xdt[g, pr]
        bd = jnp.concatenate([jnp.where(left, xd, 0.0), jnp.where(left, 0.0, xd)], axis=0)
        ecum = jnp.where(left, jnp.exp(c0), jnp.exp(c1))
        y[g, pr] = _dot(cb2[g] * seg, bd) + y2[g][:, pr * 2 * P:(pr + 1) * 2 * P] * ecum
    for g, pr in pairs:
        c0, c1 = cc[g, pr]
        l0, l1 = c0[c - 1:c, :], c1[c - 1:c, :]
        e2 = jnp.where(left, jnp.exp(l0 - c0), jnp.exp(l1 - c1))
        upd[g, pr] = (_dot_tn(xdt[g, pr] * e2, bm[g]), jnp.where(top_s, jnp.exp(l0), jnp.exp(l1)))
    for g, pr in pairs:
        h0 = g * R + 2 * pr
        u, el = upd[g, pr]
        s_new = s_ref[h0:h0 + 2].reshape(2 * P, N) * el + u
        s_ref[h0] = s_new[:P]
        s_ref[h0 + 1] = s_new[P:]
    for g in G:
        sl = slice(g * gw, (g + 1) * gw)
        yg = jnp.concatenate([y[g, pr] for pr in range(R // 2)], axis=1)
        o_ref[:, sl] = _ssd_finish(yg, x_ref[:, sl], z_ref[:, sl], d_ref[:, sl], nw_ref[:, sl]).astype(o_ref.dtype)


def ssd_prompt(xbc, proj, z_off, g1, g2, lane0, d_exp, norm_w, batch, seq, n_heads):
    c = CHUNK
    nc = seq // c
    inner = n_heads * SSM_HEAD_DIM
    bc_w = SSM_GROUPS * SSM_D_STATE
    b_off = inner // bc_w
    zb = z_off // inner
    rowblk = lambda b, n: b * nc + n
    vec = pl.BlockSpec((1, inner), lambda b, n: (0, 0))
    gate = pl.BlockSpec((c, LANES), lambda b, n: (rowblk(b, n), 0))
    return pl.pallas_call(
        functools.partial(_ssd_prompt_kernel, n_heads=n_heads, lane0=lane0),
        out_shape=(jax.ShapeDtypeStruct((batch * seq, inner), BF16),
                   jax.ShapeDtypeStruct((batch, n_heads, SSM_HEAD_DIM, SSM_D_STATE), F32)),
        grid=(batch, nc),
        in_specs=[pl.BlockSpec((c, inner), lambda b, n: (rowblk(b, n), 0)),
                  pl.BlockSpec((c, bc_w), lambda b, n: (rowblk(b, n), b_off)),
                  pl.BlockSpec((c, bc_w), lambda b, n: (rowblk(b, n), b_off + 1)),
                  pl.BlockSpec((c, inner), lambda b, n: (rowblk(b, n), zb)),
                  gate, gate, vec, vec],
        out_specs=(pl.BlockSpec((c, inner), lambda b, n: (rowblk(b, n), 0)),
                   pl.BlockSpec((None, n_heads, SSM_HEAD_DIM, SSM_D_STATE), lambda b, n: (b, 0, 0, 0))),
        compiler_params=_params("parallel", "arbitrary"),
        name="ssd_prompt",
    )(xbc, xbc, xbc, proj, g1, g2, d_exp, norm_w)


def _ssd_step_kernel(xt_ref, xr_ref, b_ref, c_ref, z_ref, g1_ref, g2_ref, d_ref, nw_ref, s_ref,
                     *rest, n_heads, slot):
    o_ref = rest[-2]
    so_ref = _own_slot(rest[-1], slot)
    R = n_heads // SSM_GROUPS
    P = SSM_HEAD_DIM
    gw = R * P
    xt = xt_ref[0]
    g1 = g1_ref[0]
    g2 = g2_ref[0]
    off = LANES - n_heads
    for g in range(SSM_GROUPS):
        brow = b_ref[0, g:g + 1, :]
        for r in range(R):
            h = g * R + r
            dt = g1[:, off + h:off + h + 1]
            eda = jnp.exp(g2[:, off + h:off + h + 1])
            so_ref[0, h] = s_ref[0, h] * eda + (xt[:, h:h + 1] * dt) * brow
        s_new = so_ref[0, g * R:(g + 1) * R].reshape(gw, SSM_D_STATE)
        c8 = jnp.broadcast_to(c_ref[0, g:g + 1, :], (8, SSM_D_STATE))
        y = _dot_nt(c8, s_new)[0:1, :]
        sl = slice(g * gw, (g + 1) * gw)
        o_ref[0, :, sl] = _ssd_finish(y, xr_ref[0, :, sl], z_ref[0, :, sl], d_ref[:, sl],
                                      nw_ref[:, sl]).astype(o_ref.dtype)


def ssd_step(xt, x3, b3, c3, z3, g1, g2, d_exp, norm_w, state, layer, prev_out):
    nb, n_heads = state.shape[1], state.shape[2]
    inner = n_heads * SSM_HEAD_DIM
    blk3 = lambda a, b_: pl.BlockSpec((1, a, b_), lambda i: (i, 0, 0))
    vec = pl.BlockSpec((1, inner), lambda i: (0, 0))
    return _stacked_state_call(
        functools.partial(_ssd_step_kernel, n_heads=n_heads), layer, state, prev_out,
        [blk3(SSM_HEAD_DIM, n_heads), blk3(1, inner), blk3(SSM_GROUPS, SSM_D_STATE),
         blk3(SSM_GROUPS, SSM_D_STATE), blk3(1, inner), blk3(1, LANES), blk3(1, LANES), vec, vec],
        [xt, x3, b3, c3, z3, g1, g2, d_exp, norm_w],
        jax.ShapeDtypeStruct((nb, 1, inner), BF16), blk3(1, inner), (nb,), "ssd_step")


def _attn_kernel(q_ref, k_ref, v_ref, o_ref):
    d = q_ref.shape[2]
    hd = d // MEM_HEADS
    scale = hd ** -0.5
    for h in range(MEM_HEADS):
        sl = slice(h * hd, (h + 1) * hd)
        s = _dot_nt(q_ref[0, :, sl], k_ref[0, :, sl]) * scale
        s = s - jnp.max(s, axis=-1, keepdims=True)
        p = jnp.exp(s)
        p = p / jnp.sum(p, axis=-1, keepdims=True)
        o_ref[0, :, sl] = _dot(p, v_ref[0, :, sl]).astype(o_ref.dtype)


def attention(q, k, v, layer):
    b, l, d = q.shape
    m = k.shape[2]
    tl = _pick(l, (512, 256, 128, 64, 32, 16, 8, 1))
    kv = pl.BlockSpec((None, 1, m, d), lambda i, j: (layer, i, 0, 0))
    return pl.pallas_call(
        _attn_kernel,
        out_shape=jax.ShapeDtypeStruct((b, l, d), BF16),
        grid=(b, l // tl),
        in_specs=[pl.BlockSpec((1, tl, d), lambda i, j: (i, j, 0)), kv, kv],
        out_specs=pl.BlockSpec((1, tl, d), lambda i, j: (i, j, 0)),
        compiler_params=_params("parallel", "arbitrary"),
        name="attention",
    )(q, k, v)


def _router_kernel(t_ref, w_ref, b_ref, idx_ref, w8_ref, pos_ref, cnt_ref, carry_ref):
    logits = _dot_nt(w_ref[...], t_ref[...])
    scores = _sigmoid(logits)
    sel = scores + b_ref[...]
    ne, tm = sel.shape
    epg = ne // N_EXPERT_GROUPS
    neg = -jnp.inf

    s3 = sel.reshape(N_EXPERT_GROUPS, epg, tm)
    i3 = lax.broadcasted_iota(jnp.int32, s3.shape, 1)
    m1 = jnp.max(s3, axis=1, keepdims=True)
    a1 = jnp.min(jnp.where(s3 == m1, i3, epg), axis=1, keepdims=True)
    m2 = jnp.max(jnp.where(i3 == a1, neg, s3), axis=1, keepdims=True)
    grp = (m1 + m2).reshape(N_EXPERT_GROUPS, tm)

    gi = lax.broadcasted_iota(jnp.int32, grp.shape, 0)
    gsel = jnp.zeros(grp.shape, jnp.bool_)
    for _ in range(TOPK_GROUPS):
        mx = jnp.max(grp, axis=0, keepdims=True)
        am = jnp.min(jnp.where(grp == mx, gi, N_EXPERT_GROUPS), axis=0, keepdims=True)
        hit = gi == am
        gsel = gsel | hit
        grp = jnp.where(hit, neg, grp)

    gmask = jnp.broadcast_to(gsel.reshape(N_EXPERT_GROUPS, 1, tm), s3.shape).reshape(ne, tm)
    cur = jnp.where(gmask, sel, neg)
    ei = lax.broadcasted_iota(jnp.int32, cur.shape, 0)
    chosen = jnp.zeros(cur.shape, jnp.bool_)
    rounds = []
    for _ in range(TOP_K):
        mx = jnp.max(cur, axis=0, keepdims=True)
        am = jnp.min(jnp.where(cur == mx, ei, ne), axis=0, keepdims=True)
        hit = ei == am
        rounds.append((hit, am))
        chosen = chosen | hit
        cur = jnp.where(hit, neg, cur)

    wts = jnp.where(chosen, scores, 0.0)
    wn = wts / jnp.sum(wts, axis=0, keepdims=True) * ROUTED_SCALE

    @pl.when(pl.program_id(0) == 0)
    def _():
        carry_ref[...] = jnp.zeros_like(carry_ref)

    ch = jnp.where(chosen, 1.0, 0.0)
    before = lax.broadcasted_iota(jnp.int32, (tm, tm), 0) < lax.broadcasted_iota(jnp.int32, (tm, tm), 1)
    pos = carry_ref[:, 0:1] + _dot(ch, jnp.where(before, 1.0, 0.0))
    carry_ref[...] = carry_ref[...] + jnp.sum(ch, axis=1, keepdims=True)
    cnt_ref[...] = carry_ref[...]

    pick = lambda hit, a: jnp.sum(jnp.where(hit, a, 0.0), axis=0, keepdims=True)
    idx_ref[...] = jnp.concatenate([am for _, am in rounds], axis=0)
    w8_ref[...] = jnp.concatenate([pick(hit, wn) for hit, _ in rounds], axis=0)
    pos_ref[...] = jnp.concatenate([pick(hit, pos) for hit, _ in rounds], axis=0).astype(jnp.int32)


def router(t, w_t, bias_col):
    tt, d = t.shape
    ne = w_t.shape[0]
    tm = _pick(tt, (640, 512, 256, 128))
    slot = pl.BlockSpec((TOP_K, tm), lambda i: (0, i))
    return pl.pallas_call(
        _router_kernel,
        out_shape=(jax.ShapeDtypeStruct((TOP_K, tt), jnp.int32), jax.ShapeDtypeStruct((TOP_K, tt), F32),
                   jax.ShapeDtypeStruct((TOP_K, tt), jnp.int32), jax.ShapeDtypeStruct((ne, LANES), F32)),
        grid=(tt // tm,),
        in_specs=[pl.BlockSpec((tm, d), lambda i: (i, 0)),
                  pl.BlockSpec((ne, d), lambda i: (0, 0)),
                  pl.BlockSpec((ne, 1), lambda i: (0, 0))],
        out_specs=(slot, slot, slot, pl.BlockSpec((ne, LANES), lambda i: (0, 0))),
        scratch_shapes=[pltpu.VMEM((ne, LANES), F32)],
        compiler_params=_params("arbitrary"),
        name="router",
    )(t, w_t, bias_col)


def _moe_kernel(t_ref, g_ref, wg_ref, wu_ref, wd_ref, o_ref):
    e = pl.program_id(1)

    @pl.when(e == 0)
    def _():
        o_ref[...] = jnp.zeros_like(o_ref)

    t = t_ref[...]
    gate = g_ref[...]
    lane = lax.broadcasted_iota(jnp.int32, gate.shape, 1)
    gcol = jnp.sum(jnp.where(lane == e, gate, 0.0), axis=1, keepdims=True)
    hid = _silu(jnp.dot(t, wg_ref[0], preferred_element_type=F32)) * jnp.dot(
        t, wu_ref[0], preferred_element_type=F32)
    o_ref[...] += _dot(hid * gcol, wd_ref[0])


def moe_dense(t, gate, wg, wu, wd, layer):
    tt, d = t.shape
    _, ne, _, ff = wg.shape
    tm = _pick(tt, (640, 512, 256, 128))
    return pl.pallas_call(
        _moe_kernel,
        out_shape=jax.ShapeDtypeStruct((tt, d), F32),
        grid=(tt // tm, ne),
        in_specs=[pl.BlockSpec((tm, d), lambda i, e: (i, 0)),
                  pl.BlockSpec((tm, LANES), lambda i, e: (i, 0)),
                  pl.BlockSpec((None, 1, d, ff), lambda i, e: (layer, e, 0, 0)),
                  pl.BlockSpec((None, 1, d, ff), lambda i, e: (layer, e, 0, 0)),
                  pl.BlockSpec((None, 1, ff, d), lambda i, e: (layer, e, 0, 0))],
        out_specs=pl.BlockSpec((tm, d), lambda i, e: (i, 0)),
        compiler_params=_params("parallel", "arbitrary"),
        name="moe_dense",
    )(t, gate, wg, wu, wd)


EXPERT_TILE = 256
TOKEN_TILE = 128


def _row_copy(src, s_row, dst, d_row, sem):
    return pltpu.make_async_copy(src.at[pl.ds(s_row, 1), :], dst.at[pl.ds(d_row, 1), :], sem)


def _dispatch_kernel(dest_ref, x_ref, xs_hbm, sem):
    def issue(t, carry):
        for k in range(TOP_K):
            _row_copy(x_ref, t, xs_hbm, dest_ref[k, t], sem).start()
        return carry

    lax.fori_loop(0, TOKEN_TILE, issue, 0)

    def drain(t, carry):
        for k in range(TOP_K):
            _row_copy(x_ref, 0, xs_hbm, 0, sem).wait()
        return carry

    lax.fori_loop(0, TOKEN_TILE, drain, 0)


def dispatch_rows(dest, x, n_rows):
    tt, d = x.shape
    return pl.pallas_call(
        _dispatch_kernel,
        out_shape=jax.ShapeDtypeStruct((n_rows, d), x.dtype),
        grid=(tt // TOKEN_TILE,),
        in_specs=[pl.BlockSpec((TOP_K, TOKEN_TILE), lambda i: (0, i), memory_space=pltpu.SMEM),
                  pl.BlockSpec((TOKEN_TILE, d), lambda i: (i, 0))],
        out_specs=pl.BlockSpec(memory_space=pl.ANY),
        scratch_shapes=[pltpu.SemaphoreType.DMA],
        compiler_params=_params("arbitrary"),
        name="dispatch_rows",
    )(dest, x)


def _experts_kernel(it_ref, ie_ref, lo_ref, hi_ref, x_ref, wg_ref, wu_ref, wd_ref, y_ref,
                    wgb_ref, wub_ref, wdb_ref):
    i = pl.program_id(0)
    p = jnp.maximum(i - 1, 0)

    @pl.when((i == 0) | (ie_ref[i] != ie_ref[p]))
    def _():
        for src, dst in ((wg_ref, wgb_ref), (wu_ref, wub_ref), (wd_ref, wdb_ref)):
            for r in range(0, src.shape[0], CAST_ROWS):
                dst[r:r + CAST_ROWS, :] = src[r:r + CAST_ROWS, :].astype(BF16)

    first = (i == 0) | (it_ref[i] != it_ref[p])

    @pl.when(first)
    def _():
        y_ref[...] = jnp.zeros_like(y_ref)

    @pl.when(hi_ref[i] > lo_ref[i])
    def _():
        row = lax.broadcasted_iota(jnp.int32, (x_ref.shape[0], 1), 0)
        mine = (row >= lo_ref[i]) & (row < hi_ref[i])
        x = jnp.where(mine, x_ref[...], 0.0).astype(BF16)
        hid = _silu(jnp.dot(x, wgb_ref[...], preferred_element_type=F32)) * jnp.dot(
            x, wub_ref[...], preferred_element_type=F32)
        y_ref[...] += _dot(hid, wdb_ref[...])


def experts_sorted(item_tile, item_expert, item_lo, item_hi, xs, wg, wu, wd, layer):
    n_rows, d = xs.shape
    ff = wg.shape[3]
    te = EXPERT_TILE
    wspec = lambda a, b_: pl.BlockSpec((None, None, a, b_), lambda i, t, e, lo, hi: (layer, e[i], 0, 0))
    rows = pl.BlockSpec((te, d), lambda i, t, e, lo, hi: (t[i], 0))
    return pl.pallas_call(
        _experts_kernel,
        out_shape=jax.ShapeDtypeStruct((n_rows, d), F32),
        grid_spec=pltpu.PrefetchScalarGridSpec(
            num_scalar_prefetch=4,
            grid=(item_tile.shape[0],),
            in_specs=[rows, wspec(d, ff), wspec(d, ff), wspec(ff, d)],
            out_specs=rows,
            scratch_shapes=[pltpu.VMEM((d, ff), BF16), pltpu.VMEM((d, ff), BF16), pltpu.VMEM((ff, d), BF16)]),
        compiler_params=_params("arbitrary"),
        name="experts_sorted",
    )(item_tile, item_expert, item_lo, item_hi, xs, wg, wu, wd)


def _combine_kernel(dest_ref, w_ref, res_ref, sh_ref, g_ref, b_ref, y_hbm, of_ref, ob_ref, buf_ref, sem, *, alpha):
    def issue(t, carry):
        for k in range(TOP_K):
            _row_copy(y_hbm, dest_ref[k, t], buf_ref.at[k], t, sem).start()
        return carry

    lax.fori_loop(0, TOKEN_TILE, issue, 0)

    def drain(t, carry):
        for k in range(TOP_K):
            _row_copy(y_hbm, 0, buf_ref.at[k], 0, sem).wait()
        return carry

    lax.fori_loop(0, TOKEN_TILE, drain, 0)

    acc = sh_ref[...]
    for k in range(TOP_K):
        acc = acc + w_ref[:, k:k + 1] * buf_ref[k]
    x = alpha * res_ref[...] + acc
    mu = jnp.mean(x, axis=-1, keepdims=True)
    xc = x - mu
    var = jnp.mean(xc * xc, axis=-1, keepdims=True)
    out = xc * lax.rsqrt(var + LN_EPS) * g_ref[...] + b_ref[...]
    of_ref[...] = out
    ob_ref[...] = out.astype(BF16)


def combine_ln(dest, w_t, res, shared, y, g, b, alpha):
    tt, d = res.shape
    row = pl.BlockSpec((TOKEN_TILE, d), lambda i: (i, 0))
    vec = pl.BlockSpec((1, d), lambda i: (0, 0))
    return pl.pallas_call(
        functools.partial(_combine_kernel, alpha=alpha),
        out_shape=(jax.ShapeDtypeStruct((tt, d), F32), jax.ShapeDtypeStruct((tt, d), BF16)),
        grid=(tt // TOKEN_TILE,),
        in_specs=[pl.BlockSpec((TOP_K, TOKEN_TILE), lambda i: (0, i), memory_space=pltpu.SMEM),
                  pl.BlockSpec((TOKEN_TILE, TOP_K), lambda i: (i, 0)), row, row, vec, vec,
                  pl.BlockSpec(memory_space=pl.ANY)],
        out_specs=(row, row),
        scratch_shapes=[pltpu.VMEM((TOP_K, TOKEN_TILE, d), F32), pltpu.SemaphoreType.DMA],
        compiler_params=_params("arbitrary"),
        name="combine_ln",
    )(dest, w_t, res, shared, g.reshape(1, d), b.reshape(1, d), y)


def _pad_lanes(a, width=LANES):
    return jnp.pad(a, [(0, 0)] * (a.ndim - 1) + [(0, width - a.shape[-1])])


def kernel(x_prompt, x_sample, mem_prompt, state_dn, state_dn_conv, state_ssm, state_ssm_conv, cache_mem_k, cache_mem_v, w_in, dn_conv_w, dn_a_log, dn_dt_bias, dn_norm_w, dn_w_out, ssm_conv_w, ssm_conv_b, ssm_a_log, ssm_dt_bias, ssm_d, ssm_norm_w, ssm_w_out, w_o, ln1_g, ln1_b, xa_wq, xa_wk, xa_wv, xa_wo, ln2_g, ln2_b, router_w, router_bias, exp_w_gate, exp_w_up, exp_w_down, sh_w_gate, sh_w_up, sh_w_down, ln3_g, ln3_b):
    bp, seq, d = x_prompt.shape
    bs = x_sample.shape[0]
    depth = w_in.shape[0]
    tp = bp * seq
    n_mem = mem_prompt.shape[1]
    hv = state_dn.shape[2]
    hk = hv // 2
    qk_w = hk * DN_HEAD
    v_w = hv * DN_HEAD
    dn_conv = 2 * qk_w + v_w
    hs = state_ssm.shape[2]
    inner = hs * SSM_HEAD_DIM
    bc_w = SSM_GROUPS * SSM_D_STATE
    ssm_conv = inner + 2 * bc_w
    ne = router_w.shape[2]
    alpha = (2 * depth) ** 0.25

    o_z, o_b = dn_conv, dn_conv + v_w
    o_xbc = o_b + 2 * hv
    o_sz = o_xbc + ssm_conv
    o_dt = o_sz + inner
    o_gate = o_dt + hs
    assert o_b % LANES == 0 and 2 * hv + hs == LANES and (o_dt + hs) % LANES == 0

    h_f32 = jnp.concatenate([x_prompt.reshape(tp, d), x_sample.reshape(bs, d)], axis=0)
    h_bf = h_f32.astype(BF16)
    mem_bf = mem_prompt.reshape(bp * n_mem, d).astype(BF16)
    mem_k_cache = cache_mem_k.reshape(depth, bs, n_mem, d)
    mem_v_cache = cache_mem_v.reshape(depth, bs, n_mem, d)
    hd = d // MEM_HEADS

    s_wg, s_wu, s_wd = (w[:, None].astype(BF16) for w in (sh_w_gate, sh_w_up, sh_w_down))
    ones_gate = jnp.ones((tp + bs, LANES), F32)
    n_rows = (tp + bs) * TOP_K
    n_tiles = n_rows // EXPERT_TILE

    outs = {k: [] for k in ("dn_p", "dnc_p", "ssm_p", "ssmc_p", "mk_p", "mv_p", "dnc_s", "ssmc_s")}
    dn_s_all = None
    ssm_s_all = None

    for i in range(depth):
        p_dn = mm(h_bf, w_in, i, 0, o_b)
        p_ba = mm(h_bf, w_in, i, o_b, LANES)
        p_xbc = mm(h_bf, w_in, i, o_xbc, ssm_conv)
        p_sz = mm(h_bf, w_in, i, o_sz, inner)
        p_dt = mm(h_bf, w_in, i, o_dt + hs - LANES, LANES)
        p_gate = mm(h_bf, w_in, i, o_gate, 2 * d)

        zeros_hv = jnp.zeros((hv,), F32)
        bias_row = jnp.concatenate([zeros_hv, dn_dt_bias[i], ssm_dt_bias[i]]).reshape(1, LANES)
        alog_row = jnp.concatenate([zeros_hv, dn_a_log[i], ssm_a_log[i]]).reshape(1, LANES)
        g1, g2 = gates(p_ba, p_dt, bias_row, alog_row, hv, hv)

        dn_scale = jnp.concatenate([jnp.full((qk_w,), DN_HEAD ** -0.5, F32), jnp.ones((qk_w,), F32)]).reshape(1, -1)
        dn_zero_b = jnp.zeros((1, dn_conv), F32)
        cw = dn_conv_w[i]
        qk_p = conv_prompt(p_dn, 0, 2 * qk_w, seq, bp, cw[:, :2 * qk_w], dn_zero_b[:, :2 * qk_w], dn_scale, True)
        v_p = conv_prompt(p_dn, 2 * qk_w, v_w, seq, bp, cw[:, 2 * qk_w:], dn_zero_b[:, 2 * qk_w:],
                          dn_zero_b[:, 2 * qk_w:], False)
        scw = ssm_conv_w[i]
        scb = ssm_conv_b[i].reshape(1, ssm_conv)
        xbc_p = conv_prompt(p_xbc, 0, ssm_conv, seq, bp, scw, scb, scb, False)

        dn_buf_t = jnp.transpose(state_dn_conv[i], (1, 0, 2))
        qk_s = conv_step(p_dn, tp, bs, 0, 2 * qk_w, dn_buf_t[:, :, :2 * qk_w], cw[:, :2 * qk_w],
                         dn_zero_b[:, :2 * qk_w], dn_scale, True)
        v_s = conv_step(p_dn, tp, bs, 2 * qk_w, v_w, dn_buf_t[:, :, 2 * qk_w:], cw[:, 2 * qk_w:],
                        dn_zero_b[:, 2 * qk_w:], dn_zero_b[:, 2 * qk_w:], False)
        ssm_buf_t = jnp.transpose(state_ssm_conv[i], (1, 0, 2))
        xbc_s = conv_step(p_xbc, tp, bs, 0, ssm_conv, ssm_buf_t, scw, scb, scb, False)

        tail = seq - (CONV_WIDTH - 1)
        last_rows = lambda p, w: jnp.stack([p[b * seq + tail:(b + 1) * seq, :w] for b in range(bp)])
        outs["dnc_p"].append(last_rows(p_dn, dn_conv))
        outs["ssmc_p"].append(last_rows(p_xbc, ssm_conv))
        outs["dnc_s"].append(jnp.concatenate([state_dn_conv[i][:, 1:], p_dn[tp:, None, :dn_conv]], axis=1))
        outs["ssmc_s"].append(jnp.concatenate([state_ssm_conv[i][:, 1:], p_xbc[tp:, None]], axis=1))

        nw_dn = dn_norm_w[i].reshape(1, DN_HEAD)
        o_dn_p, s_dn_p = delta_prompt(qk_p, v_p, p_dn, o_z, g1, nw_dn, bp, seq, hv)
        qt = jnp.transpose(qk_s[:, :qk_w].reshape(bs, hk, DN_HEAD), (0, 2, 1))
        kt = jnp.transpose(qk_s[:, qk_w:].reshape(bs, hk, DN_HEAD), (0, 2, 1))
        z_dn_s = p_dn[tp:, o_z:].reshape(bs, 1, v_w)
        o_dn_s, dn_s_all = delta_step(qt, kt, v_s.reshape(bs, 1, v_w), z_dn_s, g1[tp:].reshape(bs, 1, LANES),
                                      nw_dn, state_dn, i, dn_s_all)
        o_dn = jnp.concatenate([o_dn_p, o_dn_s.reshape(bs, v_w)], axis=0)
        outs["dn_p"].append(s_dn_p)

        d_exp = jnp.repeat(ssm_d[i], SSM_HEAD_DIM).reshape(1, inner)
        nw_ssm = ssm_norm_w[i].reshape(1, inner)
        y_p, s_ssm_p = ssd_prompt(xbc_p, p_sz, 0, g1, g2, 2 * hv, d_exp, nw_ssm, bp, seq, hs)
        xs_s = xbc_s[:, :inner]
        xt_s = jnp.transpose(xs_s.reshape(bs, hs, SSM_HEAD_DIM), (0, 2, 1))
        b_s = xbc_s[:, inner:inner + bc_w].reshape(bs, SSM_GROUPS, SSM_D_STATE)
        c_s = xbc_s[:, inner + bc_w:].reshape(bs, SSM_GROUPS, SSM_D_STATE)
        y_s, ssm_s_all = ssd_step(xt_s, xs_s.reshape(bs, 1, inner), b_s, c_s, p_sz[tp:].reshape(bs, 1, inner),
                                  g1[tp:].reshape(bs, 1, LANES), g2[tp:].reshape(bs, 1, LANES),
                                  d_exp, nw_ssm, state_ssm, i, ssm_s_all)
        y_ssm = jnp.concatenate([y_p, y_s.reshape(bs, inner)], axis=0)
        outs["ssm_p"].append(s_ssm_p)

        y_a = mm(o_dn, dn_w_out, i)
        y_b = mm(y_ssm, ssm_w_out, i)
        merged = gate_merge(p_gate, 0, y_a, y_b)
        h_f32, h_bf = ln_res(h_f32, [mm(merged, w_o, i)], ln1_g[i], ln1_b[i], alpha)

        mk = mm(mem_bf, xa_wk, i)
        mv = mm(mem_bf, xa_wv, i)
        outs["mk_p"].append(mk.reshape(bp, n_mem, MEM_HEADS, hd))
        outs["mv_p"].append(mv.reshape(bp, n_mem, MEM_HEADS, hd))
        q = mm(h_bf, xa_wq, i, out_dtype=BF16)
        a_p = attention(q[:tp].reshape(bp, seq, d), mk.reshape(1, bp, n_mem, d), mv.reshape(1, bp, n_mem, d), 0)
        a_s = attention(q[tp:].reshape(bs, 1, d), mem_k_cache, mem_v_cache, i)
        att = jnp.concatenate([a_p.reshape(tp, d), a_s.reshape(bs, d)], axis=0)
        h_f32, h_bf = ln_res(h_f32, [mm(att, xa_wo, i)], ln2_g[i], ln2_b[i], alpha)

        idx8, w8, pos8, cnt = router(h_f32, jnp.transpose(router_w[i]), router_bias[i].reshape(ne, 1))
        cnt = cnt[:, 0].astype(jnp.int32)
        end = jnp.cumsum(cnt)
        start = end - cnt
        e_ids = jnp.arange(ne, dtype=jnp.int32)
        dest = pos8 + jnp.sum(jnp.where(idx8[:, :, None] == e_ids, start, 0), axis=-1)
        vals = jnp.concatenate([jnp.arange(n_tiles, dtype=jnp.int32) * EXPERT_TILE, end[:-1]])
        n_items = vals.shape[0]
        ids = jnp.arange(n_items, dtype=jnp.int32)
        below = (vals[None, :] < vals[:, None]) | ((vals[None, :] == vals[:, None]) & (ids[None, :] < ids[:, None]))
        rank = jnp.sum(below, axis=1).astype(jnp.int32)
        cuts = jnp.sum(jnp.where(rank[:, None] == ids[None, :], vals[:, None], 0), axis=0)
        nxt = jnp.concatenate([cuts[1:], jnp.full((1,), n_rows, jnp.int32)])
        item_tile = jnp.minimum(cuts // EXPERT_TILE, n_tiles - 1)
        item_e = jnp.minimum(jnp.sum(end[None, :] <= cuts[:, None], axis=1), ne - 1).astype(jnp.int32)
        item_lo = cuts - item_tile * EXPERT_TILE
        item_hi = nxt - item_tile * EXPERT_TILE

        xs = dispatch_rows(dest, h_f32, n_rows)
        ys = experts_sorted(item_tile, item_e, item_lo, item_hi, xs, exp_w_gate, exp_w_up, exp_w_down, i)
        shared = moe_dense(h_bf, ones_gate, s_wg, s_wu, s_wd, i)
        h_f32, h_bf = combine_ln(dest, jnp.transpose(w8), h_f32, shared, ys, ln3_g[i], ln3_b[i], alpha)

    st = lambda k: jnp.stack(outs[k])
    return (h_f32[:tp].reshape(bp, seq, d).astype(x_prompt.dtype),
            h_f32[tp:].reshape(bs, 1, d).astype(x_sample.dtype),
            st("dn_p"), st("dnc_p"), st("ssm_p"), st("ssmc_p"), st("mk_p"), st("mv_p"),
            dn_s_all, st("dnc_s"), ssm_s_all, st("ssmc_s"))
```
